```python
import jax, jax.numpy as jnp
from jax import lax
import numpy as np

D_MODEL = 1024
BATCH = 16
SEQ = 256
DEPTH = 4
DEC_BATCH = 8
DEC_SEQ = 2048
PAST_LEN = 512

GRID_W = 64
N_EVEN = (DEPTH + 1) // 2
N_ODD = DEPTH // 2
A_WIDTH = D_MODEL // 2
HEAD_DIM = 64
N_HEADS_A = A_WIDTH // HEAD_DIM
WIN_H_MAX = 8
WIN_W = 16
B_WIDTH = D_MODEL - A_WIDTH
POOL_WINDOWS = (2, 4, 8, 16)
N_POOL = len(POOL_WINDOWS)
POOL_GROUP = B_WIDTH // N_POOL
C_WIDTH = D_MODEL
CONV_K = 31
D_FF = 2816
FFN_CONV_K = 3
N_MOD = 6
Q_BLOCK = 128
EPS = 1e-6
NEG_INF = -1e30

kernel_name = "hybrid_natten_pool_conformer_dit_step"


def rmsnorm(x, g):
    xf = x.astype(jnp.float32)
    y = xf * lax.rsqrt(jnp.mean(xf * xf, axis=-1, keepdims=True) + EPS)
    return (y * g.astype(jnp.float32)).astype(x.dtype)


def layernorm(x, g, b):
    xf = x.astype(jnp.float32)
    mu = jnp.mean(xf, axis=-1, keepdims=True)
    var = jnp.mean(jnp.square(xf - mu), axis=-1, keepdims=True)
    y = (xf - mu) * lax.rsqrt(var + EPS)
    return (y * g.astype(jnp.float32) + b.astype(jnp.float32)).astype(x.dtype)


def dwconv(x, w, b):
    k = w.shape[0]
    y = lax.conv_general_dilated(
        x, w[:, None, :].astype(x.dtype), window_strides=(1,),
        padding=[(k // 2, k // 2)], dimension_numbers=("NWC", "WIO", "NWC"),
        feature_group_count=x.shape[-1])
    return y + b.astype(x.dtype)


def multiscale_pool(p, pool_w, pool_scale):
    s = p.shape[1]
    pf = p.astype(jnp.float32)
    csum = jnp.concatenate([jnp.zeros_like(pf[:, :1]), jnp.cumsum(pf, axis=1)], axis=1)
    t = jnp.arange(s)
    outs = []
    for g, w in enumerate(POOL_WINDOWS):
        lo = jnp.clip(t - w // 2, 0, s - 1)
        hi = jnp.clip(t - w // 2 + w - 1, 0, s - 1)
        sl = slice(g * POOL_GROUP, (g + 1) * POOL_GROUP)
        cg = csum[..., sl]
        cnt = (hi - lo + 1).astype(jnp.float32)[None, :, None]
        outs.append((cg[:, hi + 1] - cg[:, lo]) / cnt - pf[..., sl])
    d = jnp.stack(outs, axis=2)
    y = jnp.einsum("bsgc,gcd->bsgd", d, pool_w.astype(jnp.float32))
    y = y.reshape(p.shape) * pool_scale.astype(jnp.float32)
    return y.astype(p.dtype)


def context_attention(q, k, v):
    b, l, h, dh = q.shape
    nb = l // Q_BLOCK
    qb = q.reshape(b, nb, Q_BLOCK, h, dh).transpose(1, 0, 2, 3, 4)
    scale = dh ** -0.5

    def blk(qi):
        s = jnp.einsum("bqhd,bkhd->bhqk", qi, k).astype(jnp.float32) * scale
        pr = jax.nn.softmax(s, axis=-1).astype(v.dtype)
        return jnp.einsum("bhqk,bkhd->bqhd", pr, v)

    o = lax.map(blk, qb)
    return o.transpose(1, 0, 2, 3, 4).reshape(b, l, h, dh)


def neighbourhood_attention(q, k, v, ck, cv, rpb):
    b, s, h, dh = q.shape
    rows = s // GRID_W
    kh = min(WIN_H_MAX, rows)
    kw = min(WIN_W, GRID_W)
    scale = dh ** -0.5
    r = jnp.arange(rows)
    c = jnp.arange(GRID_W)
    row_start = jnp.clip(r - kh // 2, 0, rows - kh)
    col_start = jnp.clip(c - kw // 2, 0, GRID_W - kw)
    col_valid = (c[None, :] >= col_start[:, None]) & (c[None, :] < col_start[:, None] + kw)
    drow = row_start[:, None] + jnp.arange(kh)[None, :] - r[:, None] + (WIN_H_MAX - 1)
    dcol = jnp.clip(c[None, :] - c[:, None], -(kw - 1), kw - 1) + (WIN_W - 1)
    bias = rpb.astype(jnp.float32)[:, drow[:, None, :, None], dcol[None, :, None, :]]
    bias = jnp.where(col_valid[None, None, :, None, :], bias, NEG_INF)
    bias = bias.transpose(1, 0, 2, 3, 4).reshape(rows, h, GRID_W, kh * GRID_W)
    kg = k.reshape(b, rows, GRID_W, h, dh)
    vg = v.reshape(b, rows, GRID_W, h, dh)
    qr = q.reshape(b, rows, GRID_W, h, dh).transpose(1, 0, 2, 3, 4)
    n_local = kh * GRID_W

    def row(args):
        qi, rs, bi = args
        kb = lax.dynamic_slice_in_dim(kg, rs, kh, axis=1).reshape(b, n_local, h, dh)
        vb = lax.dynamic_slice_in_dim(vg, rs, kh, axis=1).reshape(b, n_local, h, dh)
        s_loc = jnp.einsum("bqhd,bkhd->bhqk", qi, kb).astype(jnp.float32) * scale + bi[None]
        s_ctx = jnp.einsum("bqhd,bkhd->bhqk", qi, ck).astype(jnp.float32) * scale
        pr = jax.nn.softmax(jnp.concatenate([s_loc, s_ctx], axis=-1), axis=-1).astype(v.dtype)
        return (jnp.einsum("bhqk,bkhd->bqhd", pr[..., :n_local], vb)
                + jnp.einsum("bhqk,bkhd->bqhd", pr[..., n_local:], cv))

    o = lax.map(row, (qr, row_start, bias))
    return o.transpose(1, 0, 2, 3, 4).reshape(b, s, h, dh)


def even_project(h, w_in, q_gain, k_gain):
    b, s, _ = h.shape
    proj = h @ w_in
    q = rmsnorm(proj[..., :A_WIDTH].reshape(b, s, N_HEADS_A, HEAD_DIM), q_gain)
    k = rmsnorm(proj[..., A_WIDTH:2 * A_WIDTH].reshape(b, s, N_HEADS_A, HEAD_DIM), k_gain)
    v = proj[..., 2 * A_WIDTH:3 * A_WIDTH].reshape(b, s, N_HEADS_A, HEAD_DIM)
    p = proj[..., 3 * A_WIDTH:]
    return q, k, v, p


def even_output(attn, p, pool_w, pool_scale, w_out):
    b, s = p.shape[:2]
    y = jnp.concatenate([attn.reshape(b, s, A_WIDTH), multiscale_pool(p, pool_w, pool_scale)], axis=-1)
    return y @ w_out


def conformer_conv(h, w_pw1, dw_w, dw_b, ln_g, ln_b, w_pw2):
    u = h @ w_pw1
    a, g = jnp.split(u, 2, axis=-1)
    u = a * jax.nn.sigmoid(g)
    u = dwconv(u, dw_w, dw_b)
    u = jax.nn.silu(layernorm(u, ln_g, ln_b))
    return u @ w_pw2


def conv_ffn(h, w_in, conv_w, conv_b, w_out):
    u = dwconv(h @ w_in, conv_w, conv_b)
    a, g = jnp.split(u, 2, axis=-1)
    return (jax.nn.silu(a) * g) @ w_out


def setup_inputs(seed: int = 0) -> dict:
    key = jax.random.key(seed)
    ks = jax.random.split(key, 32)
    f32 = jnp.float32

    def nrm(k, shape, s):
        return jax.random.normal(k, shape, f32) * s

    def gain(k, shape):
        return 1.0 + 0.02 * jax.random.normal(k, shape, f32)

    D = D_MODEL
    return {
        "x_prompt": nrm(ks[0], (BATCH, SEQ, D), 1.0),
        "x_sample": nrm(ks[1], (DEC_BATCH, DEC_SEQ, D), 1.0),
        "cache_k": nrm(ks[2], (DEC_BATCH, N_EVEN, PAST_LEN, N_HEADS_A, HEAD_DIM), 1.0),
        "cache_v": nrm(ks[3], (DEC_BATCH, N_EVEN, PAST_LEN, N_HEADS_A, HEAD_DIM), 1.0),
        "c": nrm(ks[4], (DEC_BATCH, D), 1.0),
        "c_ctx": nrm(ks[5], (D,), 1.0),
        "norm_mix": gain(ks[6], (DEPTH, D)),
        "norm_ffn": gain(ks[7], (DEPTH, D)),
        "w_mod": nrm(ks[8], (DEPTH, D, N_MOD * D), 0.5 * D ** -0.5),
        "b_mod": nrm(ks[9], (DEPTH, N_MOD * D), 0.02),
        "w_in_ab": nrm(ks[10], (N_EVEN, D, 3 * A_WIDTH + B_WIDTH), D ** -0.5),
        "q_gain": gain(ks[11], (N_EVEN, HEAD_DIM)),
        "k_gain": gain(ks[12], (N_EVEN, HEAD_DIM)),
        "rpb": nrm(ks[13], (N_EVEN, N_HEADS_A, 2 * WIN_H_MAX - 1, 2 * WIN_W - 1), 0.1),
        "pool_w": nrm(ks[14], (N_EVEN, N_POOL, POOL_GROUP, POOL_GROUP), POOL_GROUP ** -0.5),
        "pool_scale": gain(ks[15], (N_EVEN, B_WIDTH)),
        "w_out_ab": nrm(ks[16], (N_EVEN, A_WIDTH + B_WIDTH, D), (A_WIDTH + B_WIDTH) ** -0.5),
        "conv_pw1": nrm(ks[17], (N_ODD, D, 2 * C_WIDTH), D ** -0.5),
        "conv_dw": nrm(ks[18], (N_ODD, CONV_K, C_WIDTH), CONV_K ** -0.5),
        "conv_dw_b": nrm(ks[19], (N_ODD, C_WIDTH), 0.02),
        "conv_ln_g": gain(ks[20], (N_ODD, C_WIDTH)),
        "conv_ln_b": nrm(ks[21], (N_ODD, C_WIDTH), 0.02),
        "conv_pw2": nrm(ks[22], (N_ODD, C_WIDTH, D), C_WIDTH ** -0.5),
        "ffn_w_in": nrm(ks[23], (DEPTH, D, 2 * D_FF), D ** -0.5),
        "ffn_conv_w": nrm(ks[24], (DEPTH, FFN_CONV_K, 2 * D_FF), FFN_CONV_K ** -0.5),
        "ffn_conv_b": nrm(ks[25], (DEPTH, 2 * D_FF), 0.02),
        "ffn_w_out": nrm(ks[26], (DEPTH, D_FF, D), D_FF ** -0.5),
    }


def reference(x_prompt, x_sample, cache_k, cache_v, c, c_ctx, norm_mix, norm_ffn, w_mod, b_mod,
              w_in_ab, q_gain, k_gain, rpb, pool_w, pool_scale, w_out_ab,
              conv_pw1, conv_dw, conv_dw_b, conv_ln_g, conv_ln_b, conv_pw2,
              ffn_w_in, ffn_conv_w, ffn_conv_b, ffn_w_out):
    xc = x_prompt
    xl = x_sample
    sc = jax.nn.silu(c_ctx)[None, None, :]
    sl = jax.nn.silu(c)[:, None, :]
    new_k, new_v = [], []
    for i in range(DEPTH):
        mc = jnp.split(sc @ w_mod[i] + b_mod[i], N_MOD, axis=-1)
        ml = jnp.split(sl @ w_mod[i] + b_mod[i], N_MOD, axis=-1)
        hc = rmsnorm(xc, norm_mix[i]) * (1 + mc[1]) + mc[0]
        hl = rmsnorm(xl, norm_mix[i]) * (1 + ml[1]) + ml[0]
        j = i // 2
        if i % 2 == 0:
            qc, kc, vc, pc = even_project(hc, w_in_ab[j], q_gain[j], k_gain[j])
            ql, kl, vl, pl = even_project(hl, w_in_ab[j], q_gain[j], k_gain[j])
            ac = context_attention(qc, kc, vc)
            al = neighbourhood_attention(ql, kl, vl, cache_k[:, j], cache_v[:, j], rpb[j])
            yc = even_output(ac, pc, pool_w[j], pool_scale[j], w_out_ab[j])
            yl = even_output(al, pl, pool_w[j], pool_scale[j], w_out_ab[j])
            new_k.append(kc)
            new_v.append(vc)
        else:
            yc = conformer_conv(hc, conv_pw1[j], conv_dw[j], conv_dw_b[j], conv_ln_g[j], conv_ln_b[j], conv_pw2[j])
            yl = conformer_conv(hl, conv_pw1[j], conv_dw[j], conv_dw_b[j], conv_ln_g[j], conv_ln_b[j], conv_pw2[j])
        xc = xc + mc[2] * yc
        xl = xl + ml[2] * yl
        hc = rmsnorm(xc, norm_ffn[i]) * (1 + mc[4]) + mc[3]
        hl = rmsnorm(xl, norm_ffn[i]) * (1 + ml[4]) + ml[3]
        xc = xc + mc[5] * conv_ffn(hc, ffn_w_in[i], ffn_conv_w[i], ffn_conv_b[i], ffn_w_out[i])
        xl = xl + ml[5] * conv_ffn(hl, ffn_w_in[i], ffn_conv_w[i], ffn_conv_b[i], ffn_w_out[i])
    new_cache_k = jnp.stack(new_k, axis=1)
    new_cache_v = jnp.stack(new_v, axis=1)
    return (xc, xl, new_cache_k, new_cache_v)
```

```python
import functools

import jax
import jax.numpy as jnp
from jax import lax
from jax.experimental import pallas as pl
from jax.experimental.pallas import tpu as pltpu

D = 1024
DEPTH = 4
GRID_W = 64
A_WIDTH = 512
HEAD_DIM = 64
N_HEADS = 8
WIN_H = 8
WIN_W = 16
POOL_WINDOWS = (2, 4, 8, 16)
POOL_GROUP = 128
CONV_K = 31
D_FF = 2816
EPS = 1e-6
NEG_INF = -1e30

F32 = jnp.float32
BF = jnp.bfloat16

LANES = 128
FF_CHUNK = 256
VMEM_LIMIT = 56 * 1024 * 1024


def _cparams(n_grid):
    return pltpu.CompilerParams(
        dimension_semantics=("arbitrary",) * n_grid, vmem_limit_bytes=VMEM_LIMIT)


def _resident(shape, index_map):
    return pl.BlockSpec(shape, index_map, pipeline_mode=pl.Buffered(1))


def _sigmoid(x):
    return 1.0 / (1.0 + jnp.exp(-x))


def _modnorm(x, g, shift, scale):
    ms = jnp.mean(x * x, axis=-1, keepdims=True)
    return (x * lax.rsqrt(ms + EPS) * g) * (1.0 + scale) + shift


def _mod_kernel(c_ref, w_ref, b_ref, o_ref):
    c = c_ref[...]
    s = c * _sigmoid(c)
    o_ref[...] = jnp.dot(s.astype(BF), w_ref[...].astype(BF),
                         preferred_element_type=F32) + b_ref[...]


def _modulation(c_all, w_mod, b_mod):
    tn = 1536
    return pl.pallas_call(
        _mod_kernel,
        grid=(DEPTH, 6 * D // tn),
        in_specs=[
            pl.BlockSpec((16, D), lambda l, n: (0, 0)),
            pl.BlockSpec((None, D, tn), lambda l, n: (l, 0, n)),
            pl.BlockSpec((None, 1, tn), lambda l, n: (l, 0, n)),
        ],
        out_specs=pl.BlockSpec((None, 16, tn), lambda l, n: (l, 0, n)),
        out_shape=jax.ShapeDtypeStruct((DEPTH, 16, 6 * D), F32),
        compiler_params=_cparams(2),
        name="modulation",
    )(c_all, w_mod, b_mod.reshape(DEPTH, 1, 6 * D))


def _bias_kernel(rpb_ref, o_ref, t_ref):
    j = pl.program_id(0)
    q = lax.broadcasted_iota(jnp.int32, (GRID_W, LANES), 0)
    l = lax.broadcasted_iota(jnp.int32, (GRID_W, LANES), 1)
    ck = l & (GRID_W - 1)
    dc = ck - q + (WIN_W - 1)
    cs = jnp.clip(q - WIN_W // 2, 0, GRID_W - WIN_W)
    valid = (ck >= cs) & (ck < cs + WIN_W)
    n_rho = 2 * WIN_H - 1
    n_d = 2 * WIN_W - 1

    def body(hr, carry):
        base = (j * (N_HEADS * n_rho) + hr) * n_d
        acc = jnp.full((GRID_W, LANES), NEG_INF, F32)
        for d in range(n_d):
            acc = jnp.where(valid & (dc == d), rpb_ref[base + d], acc)
        t_ref[hr] = acc
        return carry

    lax.fori_loop(0, N_HEADS * n_rho, body, 0)
    lo = l < GRID_W
    for delta in range(WIN_H):
        for h in range(N_HEADS):
            for pr in range(WIN_H // 2):
                r0 = 2 * pr - delta + (WIN_H - 1)
                o_ref[delta, h, :, pr * LANES:(pr + 1) * LANES] = jnp.where(
                    lo, t_ref[h * n_rho + r0], t_ref[h * n_rho + r0 + 1])


def _bias_table(rpb):
    n_even = rpb.shape[0]
    return pl.pallas_call(
        _bias_kernel,
        grid=(n_even,),
        in_specs=[pl.BlockSpec(memory_space=pltpu.SMEM)],
        out_specs=pl.BlockSpec((None, WIN_H, N_HEADS, GRID_W, WIN_H * GRID_W),
                               lambda j: (j, 0, 0, 0, 0)),
        out_shape=jax.ShapeDtypeStruct((n_even, WIN_H, N_HEADS, GRID_W, WIN_H * GRID_W), F32),
        scratch_shapes=[pltpu.VMEM((N_HEADS * (2 * WIN_H - 1), GRID_W, LANES), F32)],
        compiler_params=_cparams(1),
        name="bias_table",
    )(rpb.reshape(-1))


def _even_in_kernel(x_ref, mod_ref, nmix_ref, w_ref, bd_ref, qg_ref, kg_ref,
                    q_ref, k_ref, v_ref, p_ref, *cache_refs, layer, j):
    g = nmix_ref[layer:layer + 1, :]
    h = _modnorm(x_ref[...], g, mod_ref[0:1, :], mod_ref[1:2, :]).astype(BF)
    proj = jnp.dot(h, w_ref[...], preferred_element_type=F32)

    def headnorm(t, gain):
        ss = jnp.dot((t * t).astype(BF), bd_ref[...], preferred_element_type=F32)
        return t * lax.rsqrt(ss * (1.0 / HEAD_DIM) + EPS) * gain

    q = headnorm(proj[:, 0:A_WIDTH], qg_ref[j:j + 1, :])
    k = headnorm(proj[:, A_WIDTH:2 * A_WIDTH], kg_ref[j:j + 1, :])
    v = proj[:, 2 * A_WIDTH:3 * A_WIDTH]
    q_ref[...] = (q * (HEAD_DIM ** -0.5)).astype(BF)
    k_ref[...] = k.astype(BF)
    v_ref[...] = v.astype(BF)
    p_ref[...] = proj[:, 3 * A_WIDTH:]
    if cache_refs:
        cache_refs[0][...] = k
        cache_refs[1][...] = v


def _even_in(x, mod4, mod_row, norm_mix, w_in, bd, qg, kg, *, layer, j, tm, emit_cache):
    n = x.shape[0]
    tok = lambda i: (i, 0)
    const = lambda i: (0, 0)
    out_shape = [jax.ShapeDtypeStruct((n, A_WIDTH), BF)] * 3 + [jax.ShapeDtypeStruct((n, A_WIDTH), F32)]
    out_specs = [pl.BlockSpec((tm, A_WIDTH), tok)] * 4
    if emit_cache:
        out_shape += [jax.ShapeDtypeStruct((n, A_WIDTH), F32)] * 2
        out_specs += [pl.BlockSpec((tm, A_WIDTH), tok)] * 2
    return pl.pallas_call(
        functools.partial(_even_in_kernel, layer=layer, j=j),
        grid=(n // tm,),
        in_specs=[
            pl.BlockSpec((tm, D), tok),
            pl.BlockSpec((None, None, 6, D), lambda i: (layer, mod_row(i), 0, 0)),
            pl.BlockSpec(norm_mix.shape, const),
            _resident((None, D, 4 * A_WIDTH), lambda i: (j, 0, 0)),
            _resident((A_WIDTH, A_WIDTH), const),
            pl.BlockSpec(qg.shape, const),
            pl.BlockSpec(kg.shape, const),
        ],
        out_specs=out_specs,
        out_shape=out_shape,
        compiler_params=_cparams(1),
        name="even_in",
    )(x, mod4, norm_mix, w_in, bd, qg, kg)


def _dot_nt(a, b):
    return lax.dot_general(a, b, (((1,), (1,)), ((), ())), preferred_element_type=F32)


def _ctx_attn_kernel(q_ref, k_ref, v_ref, o_ref):
    lo = lax.broadcasted_iota(jnp.int32, (1, LANES), 1) < HEAD_DIM
    for pr in range(N_HEADS // 2):
        sl = slice(pr * LANES, (pr + 1) * LANES)
        q2 = q_ref[:, sl]
        k2 = k_ref[:, sl]
        v2 = v_ref[:, sl]
        outs = []
        for e in range(2):
            msk = lo if e == 0 else jnp.logical_not(lo)
            qm = jnp.where(msk, q2, jnp.zeros_like(q2))
            s = _dot_nt(qm, k2)
            m = jnp.max(s, axis=-1, keepdims=True)
            pe = jnp.exp(s - m)
            den = jnp.sum(pe, axis=-1, keepdims=True)
            outs.append(jnp.dot(pe.astype(BF), v2, preferred_element_type=F32) / den)
        o_ref[:, sl] = jnp.where(lo, outs[0], outs[1]).astype(o_ref.dtype)


def _ctx_attention(q, k, v, seq):
    n = q.shape[0]
    spec = pl.BlockSpec((seq, A_WIDTH), lambda b: (b, 0))
    return pl.pallas_call(
        _ctx_attn_kernel,
        grid=(n // seq,),
        in_specs=[spec, spec, spec],
        out_specs=spec,
        out_shape=jax.ShapeDtypeStruct((n, A_WIDTH), BF),
        compiler_params=_cparams(1),
        name="ctx_attention",
    )(q, k, v)


def _lat_attn_kernel(q_ref, k_ref, v_ref, ck_ref, cv_ref, b_ref, o_ref, *, rows):
    r = pl.program_id(1)
    rs = jnp.clip(r - WIN_H // 2, 0, rows - WIN_H)
    k0 = pl.multiple_of(rs * GRID_W, GRID_W)
    n_loc = WIN_H * GRID_W
    lo = lax.broadcasted_iota(jnp.int32, (1, LANES), 1) < HEAD_DIM
    for pr in range(N_HEADS // 2):
        sl = slice(pr * LANES, (pr + 1) * LANES)
        q2 = q_ref[:, sl]
        k2 = k_ref[pl.ds(k0, n_loc), sl]
        v2 = v_ref[pl.ds(k0, n_loc), sl]
        ck2 = ck_ref[:, sl].astype(BF)
        cv2 = cv_ref[:, sl].astype(BF)
        outs = []
        for e in range(2):
            msk = lo if e == 0 else jnp.logical_not(lo)
            qm = jnp.where(msk, q2, jnp.zeros_like(q2))
            s_loc = _dot_nt(qm, k2) + b_ref[2 * pr + e]
            s_ctx = _dot_nt(qm, ck2)
            m = jnp.maximum(jnp.max(s_loc, axis=-1, keepdims=True),
                            jnp.max(s_ctx, axis=-1, keepdims=True))
            p_loc = jnp.exp(s_loc - m)
            p_ctx = jnp.exp(s_ctx - m)
            den = jnp.sum(p_loc, axis=-1, keepdims=True) + jnp.sum(p_ctx, axis=-1, keepdims=True)
            o = (jnp.dot(p_loc.astype(BF), v2, preferred_element_type=F32)
                 + jnp.dot(p_ctx.astype(BF), cv2, preferred_element_type=F32))
            outs.append(o / den)
        o_ref[:, sl] = jnp.where(lo, outs[0], outs[1]).astype(o_ref.dtype)


def _lat_attention(q, k, v, cache_k, cache_v, bias, *, j, batch, seq):
    rows = seq // GRID_W
    past = cache_k.shape[2]

    def delta(r):
        return r - jnp.clip(r - WIN_H // 2, 0, rows - WIN_H)

    kv_spec = pl.BlockSpec((seq, A_WIDTH), lambda b, r: (b, 0))
    cache_spec = pl.BlockSpec((None, None, past, A_WIDTH), lambda b, r: (b, j, 0, 0))
    return pl.pallas_call(
        functools.partial(_lat_attn_kernel, rows=rows),
        grid=(batch, rows),
        in_specs=[
            pl.BlockSpec((GRID_W, A_WIDTH), lambda b, r: (b * rows + r, 0)),
            kv_spec, kv_spec, cache_spec, cache_spec,
            pl.BlockSpec((None, None, N_HEADS, GRID_W, WIN_H * GRID_W),
                         lambda b, r: (j, delta(r), 0, 0, 0)),
        ],
        out_specs=pl.BlockSpec((GRID_W, A_WIDTH), lambda b, r: (b * rows + r, 0)),
        out_shape=jax.ShapeDtypeStruct((batch * seq, A_WIDTH), BF),
        compiler_params=_cparams(2),
        name="lat_attention",
    )(q, k, v, cache_k, cache_v, bias)


def _even_out_kernel(x_ref, a_ref, p_ref, pp_ref, pn_ref, mod_ref, pw_ref, ps_ref, wo_ref,
                     o_ref, ext_ref, *, j, tm, tiles_per_seq, seq):
    tis = pl.program_id(0) % tiles_per_seq
    halo = pp_ref.shape[0]
    ext_ref[0:halo, :] = jnp.where(tis == 0, 0.0, pp_ref[...])
    ext_ref[halo:halo + tm, :] = p_ref[...]
    ext_ref[halo + tm:2 * halo + tm, :] = jnp.where(tis == tiles_per_seq - 1, 0.0, pn_ref[...])
    t = tis * tm + lax.broadcasted_iota(jnp.int32, (tm, 1), 0)
    pooled = []
    for g, w in enumerate(POOL_WINDOWS):
        sl = slice(g * POOL_GROUP, (g + 1) * POOL_GROUP)
        acc = None
        for off in range(-(w // 2), w // 2):
            term = ext_ref[halo + off:halo + off + tm, sl]
            acc = term if acc is None else acc + term
        cnt = (jnp.minimum(t + (w // 2 - 1), seq - 1) - jnp.maximum(t - w // 2, 0) + 1).astype(F32)
        d = acc / cnt - p_ref[:, sl]
        y = jnp.dot(d.astype(BF), pw_ref[g], preferred_element_type=F32)
        pooled.append(y * ps_ref[j:j + 1, sl])
    pooled = jnp.concatenate(pooled, axis=-1).astype(BF)
    y = (jnp.dot(a_ref[...], wo_ref[0:A_WIDTH, :], preferred_element_type=F32)
         + jnp.dot(pooled, wo_ref[A_WIDTH:, :], preferred_element_type=F32))
    o_ref[...] = x_ref[...] + mod_ref[2:3, :] * y


def _halo_specs(width, tm, n, halo):
    blocks = tm // halo
    last = n // halo - 1
    prev = pl.BlockSpec((halo, width), lambda i: (jnp.maximum(i * blocks - 1, 0), 0))
    nxt = pl.BlockSpec((halo, width), lambda i: (jnp.minimum((i + 1) * blocks, last), 0))
    return prev, nxt


def _even_out(x, attn, p, mod4, mod_row, pool_w, pool_scale, w_out, *, layer, j, tm, seq):
    n = x.shape[0]
    tok = lambda i: (i, 0)
    const = lambda i: (0, 0)
    halo = 8
    prev, nxt = _halo_specs(A_WIDTH, tm, n, halo)
    return pl.pallas_call(
        functools.partial(_even_out_kernel, j=j, tm=tm, tiles_per_seq=seq // tm, seq=seq),
        grid=(n // tm,),
        in_specs=[
            pl.BlockSpec((tm, D), tok),
            pl.BlockSpec((tm, A_WIDTH), tok),
            pl.BlockSpec((tm, A_WIDTH), tok),
            prev, nxt,
            pl.BlockSpec((None, None, 6, D), lambda i: (layer, mod_row(i), 0, 0)),
            _resident((None, len(POOL_WINDOWS), POOL_GROUP, POOL_GROUP), lambda i: (j, 0, 0, 0)),
            pl.BlockSpec(pool_scale.shape, const),
            _resident((None, D, D), lambda i: (j, 0, 0)),
        ],
        out_specs=pl.BlockSpec((tm, D), tok),
        out_shape=jax.ShapeDtypeStruct((n, D), F32),
        scratch_shapes=[pltpu.VMEM((tm + 2 * halo, A_WIDTH), F32)],
        compiler_params=_cparams(1),
        name="even_out",
    )(x, attn, p, p, p, mod4, pool_w, pool_scale, w_out)


def _conformer_kernel(x_ref, xp_ref, xn_ref, mod_ref, nmix_ref, w1_ref, dw_ref, dwb_ref,
                      lng_ref, lnb_ref, w2_ref, o_ref, h_ref, g_ref, cv_ref,
                      *, layer, j, tm, tiles_per_seq):
    tis = pl.program_id(0) % tiles_per_seq
    g = nmix_ref[layer:layer + 1, :]
    sh = mod_ref[0:1, :]
    sc = mod_ref[1:2, :]
    x = x_ref[...]
    hp = jnp.where(tis == 0, 0.0, _modnorm(xp_ref[...], g, sh, sc))
    hn = jnp.where(tis == tiles_per_seq - 1, 0.0, _modnorm(xn_ref[...], g, sh, sc))
    h_ref[...] = jnp.concatenate([hp, _modnorm(x, g, sh, sc), hn], axis=0).astype(BF)
    halo = xp_ref.shape[0]
    for c in range(D // FF_CHUNK):
        a = jnp.dot(h_ref[...], w1_ref[:, c * FF_CHUNK:(c + 1) * FF_CHUNK],
                    preferred_element_type=F32)
        gate = jnp.dot(h_ref[...], w1_ref[:, D + c * FF_CHUNK:D + (c + 1) * FF_CHUNK],
                       preferred_element_type=F32)
        g_ref[:, c * FF_CHUNK:(c + 1) * FF_CHUNK] = a * _sigmoid(gate)
    rb = 64
    base = halo - CONV_K // 2
    for c in range(D // LANES):
        sl = slice(c * LANES, (c + 1) * LANES)
        for r0 in range(0, tm, rb):
            acc = jnp.zeros((rb, LANES), F32)
            for k in range(CONV_K):
                acc = acc + dw_ref[k:k + 1, sl] * g_ref[base + r0 + k:base + r0 + k + rb, sl]
            cv_ref[r0:r0 + rb, sl] = acc + dwb_ref[j:j + 1, sl]
    u = cv_ref[...]
    mu = jnp.mean(u, axis=-1, keepdims=True)
    uc = u - mu
    var = jnp.mean(uc * uc, axis=-1, keepdims=True)
    y = uc * lax.rsqrt(var + EPS) * lng_ref[j:j + 1, :] + lnb_ref[j:j + 1, :]
    y = y * _sigmoid(y)
    out = jnp.dot(y.astype(BF), w2_ref[...], preferred_element_type=F32)
    o_ref[...] = x + mod_ref[2:3, :] * out


def _conformer(x, mod4, mod_row, norm_mix, w1, dw, dwb, lng, lnb, w2, *, layer, j, tm, seq):
    n = x.shape[0]
    tok = lambda i: (i, 0)
    const = lambda i: (0, 0)
    halo = 16
    prev, nxt = _halo_specs(D, tm, n, halo)
    return pl.pallas_call(
        functools.partial(_conformer_kernel, layer=layer, j=j, tm=tm, tiles_per_seq=seq // tm),
        grid=(n // tm,),
        in_specs=[
            pl.BlockSpec((tm, D), tok), prev, nxt,
            pl.BlockSpec((None, None, 6, D), lambda i: (layer, mod_row(i), 0, 0)),
            pl.BlockSpec(norm_mix.shape, const),
            _resident((None, D, 2 * D), lambda i: (j, 0, 0)),
            pl.BlockSpec((None, CONV_K, D), lambda i: (j, 0, 0)),
            pl.BlockSpec(dwb.shape, const),
            pl.BlockSpec(lng.shape, const),
            pl.BlockSpec(lnb.shape, const),
            _resident((None, D, D), lambda i: (j, 0, 0)),
        ],
        out_specs=pl.BlockSpec((tm, D), tok),
        out_shape=jax.ShapeDtypeStruct((n, D), F32),
        scratch_shapes=[
            pltpu.VMEM((tm + 2 * halo, D), BF),
            pltpu.VMEM((tm + 2 * halo, D), F32),
            pltpu.VMEM((tm, D), F32),
        ],
        compiler_params=_cparams(1),
        name="conformer",
    )(x, x, x, mod4, norm_mix, w1, dw, dwb, lng, lnb, w2)


def _ffn_kernel(x_ref, xp_ref, xn_ref, mod_ref, nffn_ref, wi_ref, cw_ref, cb_ref, wo_ref,
                o_ref, h_ref, act_ref, *, layer, tm, tiles_per_seq):
    tis = pl.program_id(0) % tiles_per_seq
    g = nffn_ref[layer:layer + 1, :]
    sh = mod_ref[3:4, :]
    sc = mod_ref[4:5, :]
    x = x_ref[...]
    hp = jnp.where(tis == 0, 0.0, _modnorm(xp_ref[...], g, sh, sc))
    hn = jnp.where(tis == tiles_per_seq - 1, 0.0, _modnorm(xn_ref[...], g, sh, sc))
    z = jnp.zeros_like(hp)
    h_ref[...] = jnp.concatenate([z, hp, _modnorm(x, g, sh, sc), hn, z], axis=0).astype(BF)
    n_ext = tm + 32
    top = 16

    def conv(col):
        u = jnp.dot(h_ref[...], wi_ref[:, col:col + FF_CHUNK], preferred_element_type=F32)
        um = pltpu.roll(u, 1, 0)[top:top + tm]
        up = pltpu.roll(u, n_ext - 1, 0)[top:top + tm]
        return (cw_ref[0:1, col:col + FF_CHUNK] * um
                + cw_ref[1:2, col:col + FF_CHUNK] * u[top:top + tm]
                + cw_ref[2:3, col:col + FF_CHUNK] * up
                + cb_ref[layer:layer + 1, col:col + FF_CHUNK])

    for c in range(D_FF // FF_CHUNK):
        a = conv(c * FF_CHUNK)
        gate = conv(D_FF + c * FF_CHUNK)
        act_ref[:, c * FF_CHUNK:(c + 1) * FF_CHUNK] = (a * _sigmoid(a) * gate).astype(BF)
    y = jnp.dot(act_ref[...], wo_ref[...], preferred_element_type=F32)
    o_ref[...] = x + mod_ref[5:6, :] * y


def _ffn(x, mod4, mod_row, norm_ffn, w_in, conv_w, conv_b, w_out, *, layer, tm, seq):
    n = x.shape[0]
    tok = lambda i: (i, 0)
    const = lambda i: (0, 0)
    halo = 8
    prev, nxt = _halo_specs(D, tm, n, halo)
    return pl.pallas_call(
        functools.partial(_ffn_kernel, layer=layer, tm=tm, tiles_per_seq=seq // tm),
        grid=(n // tm,),
        in_specs=[
            pl.BlockSpec((tm, D), tok), prev, nxt,
            pl.BlockSpec((None, None, 6, D), lambda i: (layer, mod_row(i), 0, 0)),
            pl.BlockSpec(norm_ffn.shape, const),
            _resident((None, D, 2 * D_FF), lambda i: (layer, 0, 0)),
            pl.BlockSpec((None, 3, 2 * D_FF), lambda i: (layer, 0, 0)),
            pl.BlockSpec(conv_b.shape, const),
            _resident((None, D_FF, D), lambda i: (layer, 0, 0)),
        ],
        out_specs=pl.BlockSpec((tm, D), tok),
        out_shape=jax.ShapeDtypeStruct((n, D), F32),
        scratch_shapes=[
            pltpu.VMEM((tm + 32, D), BF),
            pltpu.VMEM((tm, D_FF), BF),
        ],
        compiler_params=_cparams(1),
        name="conv_ffn",
    )(x, x, x, mod4, norm_ffn, w_in, conv_w, conv_b, w_out)


def kernel(x_prompt, x_sample, cache_k, cache_v, c, c_ctx, norm_mix, norm_ffn, w_mod, b_mod,
           w_in_ab, q_gain, k_gain, rpb, pool_w, pool_scale, w_out_ab,
           conv_pw1, conv_dw, conv_dw_b, conv_ln_g, conv_ln_b, conv_pw2,
           ffn_w_in, ffn_conv_w, ffn_conv_b, ffn_w_out):
    batch, seq, _ = x_prompt.shape
    dec_batch, dec_seq, _ = x_sample.shape
    n_even = w_in_ab.shape[0]
    past = cache_k.shape[2]

    xc = x_prompt.reshape(batch * seq, D)
    xl = x_sample.reshape(dec_batch * dec_seq, D)

    ctx_row = dec_batch
    c_all = jnp.concatenate([c, c_ctx[None, :], jnp.zeros((16 - dec_batch - 1, D), F32)], axis=0)
    mod4 = _modulation(c_all, w_mod, b_mod).reshape(DEPTH, 16, 6, D)
    bias = _bias_table(rpb)

    tm_c = seq
    tm_l = 512
    tm_conf = 256
    row_c = lambda i: ctx_row
    row_l = lambda tm: (lambda i: i // (dec_seq // tm))

    head_ids = jnp.arange(A_WIDTH) // HEAD_DIM
    bd = (head_ids[:, None] == head_ids[None, :]).astype(BF)
    qg = jnp.tile(q_gain, (1, N_HEADS))
    kg = jnp.tile(k_gain, (1, N_HEADS))
    ck = cache_k.reshape(dec_batch, n_even, past, A_WIDTH)
    cv = cache_v.reshape(dec_batch, n_even, past, A_WIDTH)

    w_in_ab_b = w_in_ab.astype(BF)
    w_out_ab_b = w_out_ab.astype(BF)
    pool_w_b = pool_w.astype(BF)
    pw1_b = conv_pw1.astype(BF)
    pw2_b = conv_pw2.astype(BF)
    ffn_wi_b = ffn_w_in.astype(BF)
    ffn_wo_b = ffn_w_out.astype(BF)

    new_k, new_v = [], []
    for layer in range(DEPTH):
        j = layer // 2
        if layer % 2 == 0:
            qc, kc, vc, pc, kf, vf = _even_in(xc, mod4, row_c, norm_mix, w_in_ab_b, bd, qg, kg,
                                              layer=layer, j=j, tm=tm_c, emit_cache=True)
            ql, kl, vl, p_l = _even_in(xl, mod4, row_l(tm_l), norm_mix, w_in_ab_b, bd, qg, kg,
                                       layer=layer, j=j, tm=tm_l, emit_cache=False)
            ac = _ctx_attention(qc, kc, vc, seq)
            al = _lat_attention(ql, kl, vl, ck, cv, bias, j=j, batch=dec_batch, seq=dec_seq)
            xc = _even_out(xc, ac, pc, mod4, row_c, pool_w_b, pool_scale, w_out_ab_b,
                           layer=layer, j=j, tm=tm_c, seq=seq)
            xl = _even_out(xl, al, p_l, mod4, row_l(tm_l), pool_w_b, pool_scale, w_out_ab_b,
                           layer=layer, j=j, tm=tm_l, seq=dec_seq)
            new_k.append(kf.reshape(batch, seq, N_HEADS, HEAD_DIM))
            new_v.append(vf.reshape(batch, seq, N_HEADS, HEAD_DIM))
        else:
            xc = _conformer(xc, mod4, row_c, norm_mix, pw1_b, conv_dw, conv_dw_b, conv_ln_g,
                            conv_ln_b, pw2_b, layer=layer, j=j, tm=tm_conf, seq=seq)
            xl = _conformer(xl, mod4, row_l(tm_conf), norm_mix, pw1_b, conv_dw, conv_dw_b,
                            conv_ln_g, conv_ln_b, pw2_b, layer=layer, j=j, tm=tm_conf, seq=dec_seq)
        xc = _ffn(xc, mod4, row_c, norm_ffn, ffn_wi_b, ffn_conv_w, ffn_conv_b, ffn_wo_b,
                  layer=layer, tm=tm_c, seq=seq)
        xl = _ffn(xl, mod4, row_l(tm_l), norm_ffn, ffn_wi_b, ffn_conv_w, ffn_conv_b, ffn_wo_b,
                  layer=layer, tm=tm_l, seq=dec_seq)

    return (xc.reshape(batch, seq, D), xl.reshape(dec_batch, dec_seq, D),
            jnp.stack(new_k, axis=1), jnp.stack(new_v, axis=1))
```

```python
import functools

import jax
import jax.numpy as jnp
from jax import lax
from jax.experimental import pallas as pl
from jax.experimental.pallas import tpu as pltpu

D = 1024
DEPTH = 4
GRID_W = 64
A_WIDTH = 512
HEAD_DIM = 64
N_HEADS = 8
WIN_H = 8
WIN_W = 16
POOL_WINDOWS = (2, 4, 8, 16)
POOL_GROUP = 128
CONV_K = 31
D_FF = 2816
EPS = 1e-6
NEG_INF = -1e30

F32 = jnp.float32
BF = jnp.bfloat16

LANES = 128
SUBLANES = 8
FF_CHUNK = 256
VMEM_LIMIT = 56 * 1024 * 1024


def _cparams(n_grid):
    return pltpu.CompilerParams(
        dimension_semantics=("arbitrary",) * n_grid, vmem_limit_bytes=VMEM_LIMIT)


def _resident(shape, index_map):
    return pl.BlockSpec(shape, index_map, pipeline_mode=pl.Buffered(1))


def _sigmoid(x):
    return 1.0 / (1.0 + jnp.exp(-x))


def _modnorm(x, g, shift, scale):
    ms = jnp.mean(x * x, axis=-1, keepdims=True)
    return (x * lax.rsqrt(ms + EPS) * g) * (1.0 + scale) + shift


def _mod_kernel(c_ref, w_ref, b_ref, o_ref):
    c = c_ref[...]
    s = c * _sigmoid(c)
    o_ref[...] = jnp.dot(s.astype(BF), w_ref[...].astype(BF),
                         preferred_element_type=F32) + b_ref[...]


def _modulation(c_all, w_mod, b_mod):
    tn = 1536
    return pl.pallas_call(
        _mod_kernel,
        grid=(DEPTH, 6 * D // tn),
        in_specs=[
            pl.BlockSpec((16, D), lambda l, n: (0, 0)),
            pl.BlockSpec((None, D, tn), lambda l, n: (l, 0, n)),
            pl.BlockSpec((None, 1, tn), lambda l, n: (l, 0, n)),
        ],
        out_specs=pl.BlockSpec((None, 16, tn), lambda l, n: (l, 0, n)),
        out_shape=jax.ShapeDtypeStruct((DEPTH, 16, 6 * D), F32),
        compiler_params=_cparams(2),
        name="modulation",
    )(c_all, w_mod, b_mod.reshape(DEPTH, 1, 6 * D))


Q_ROWS = WIN_H // 2
WIN_ROWS = Q_ROWS + WIN_H - 1
N_BIAS_CLASSES = 3


def _group_window_start(r0, rows):
    return jnp.clip(r0 - WIN_H // 2, 0, rows - WIN_ROWS)


def _bias_kernel(rpb_ref, o_ref, t_ref, *, rows):
    j = pl.program_id(0)
    cls = pl.program_id(1)
    q = lax.broadcasted_iota(jnp.int32, (GRID_W, LANES), 0)
    l = lax.broadcasted_iota(jnp.int32, (GRID_W, LANES), 1)
    n_rho = 2 * WIN_H - 1
    n_d = 2 * WIN_W - 1

    @pl.when(cls == 0)
    def _():
        ck = l & (GRID_W - 1)
        dc = ck - q + (WIN_W - 1)
        cs = jnp.clip(q - WIN_W // 2, 0, GRID_W - WIN_W)
        valid = (ck >= cs) & (ck < cs + WIN_W)

        def body(hr, carry):
            base = (j * (N_HEADS * n_rho) + hr) * n_d
            acc = jnp.full((GRID_W, LANES), NEG_INF, F32)
            for d in range(n_d):
                acc = jnp.where(valid & (dc == d), rpb_ref[base + d], acc)
            t_ref[hr] = acc
            return carry

        lax.fori_loop(0, N_HEADS * n_rho, body, 0)

    r0 = jnp.where(cls == 0, 0, jnp.where(cls == 1, Q_ROWS, rows - Q_ROWS))
    ws = _group_window_start(r0, rows)
    lo = l < GRID_W
    neg = jnp.full((GRID_W, LANES), NEG_INF, F32)
    for h in range(N_HEADS):
        for rq in range(Q_ROWS):
            r = r0 + rq
            rs = jnp.clip(r - WIN_H // 2, 0, rows - WIN_H)

            def blk(kappa):
                ka = ws + kappa
                ok = (ka >= rs) & (ka < rs + WIN_H)
                rho = jnp.clip(ka - r + (WIN_H - 1), 0, n_rho - 1)
                return jnp.where(ok, t_ref[h * n_rho + rho], neg)

            qs = slice(rq * GRID_W, (rq + 1) * GRID_W)
            for pr in range(WIN_ROWS // 2):
                o_ref[h, qs, pr * LANES:(pr + 1) * LANES] = jnp.where(lo, blk(2 * pr), blk(2 * pr + 1))
            if WIN_ROWS % 2:
                tail = (WIN_ROWS - 1) * GRID_W
                o_ref[h, qs, tail:tail + GRID_W] = blk(WIN_ROWS - 1)[:, 0:GRID_W]


def _bias_table(rpb, rows):
    n_even = rpb.shape[0]
    blk = (N_HEADS, Q_ROWS * GRID_W, WIN_ROWS * GRID_W)
    return pl.pallas_call(
        functools.partial(_bias_kernel, rows=rows),
        grid=(n_even, N_BIAS_CLASSES),
        in_specs=[pl.BlockSpec(memory_space=pltpu.SMEM)],
        out_specs=pl.BlockSpec((None, None) + blk, lambda j, c: (j, c, 0, 0, 0)),
        out_shape=jax.ShapeDtypeStruct((n_even, N_BIAS_CLASSES) + blk, F32),
        scratch_shapes=[pltpu.VMEM((N_HEADS * (2 * WIN_H - 1), GRID_W, LANES), F32)],
        compiler_params=_cparams(2),
        name="bias_table",
    )(rpb.reshape(-1))


def _even_in_kernel(x_ref, mod_ref, nmix_ref, w_ref, bd_ref, qg_ref, kg_ref,
                    q_ref, k_ref, v_ref, p_ref, *cache_refs, layer, j):
    g = nmix_ref[layer:layer + 1, :]
    h = _modnorm(x_ref[...], g, mod_ref[0:1, :], mod_ref[1:2, :]).astype(BF)
    proj = jnp.dot(h, w_ref[...], preferred_element_type=F32)

    def headnorm(t, gain):
        ss = jnp.dot((t * t).astype(BF), bd_ref[...], preferred_element_type=F32)
        return t * lax.rsqrt(ss * (1.0 / HEAD_DIM) + EPS) * gain

    q = headnorm(proj[:, 0:A_WIDTH], qg_ref[j:j + 1, :])
    k = headnorm(proj[:, A_WIDTH:2 * A_WIDTH], kg_ref[j:j + 1, :])
    v = proj[:, 2 * A_WIDTH:3 * A_WIDTH]
    q_ref[...] = (q * (HEAD_DIM ** -0.5)).astype(BF)
    k_ref[...] = k.astype(BF)
    v_ref[...] = v.astype(BF)
    p_ref[...] = proj[:, 3 * A_WIDTH:]
    if cache_refs:
        cache_refs[0][...] = k
        cache_refs[1][...] = v


def _even_in(x, mod4, mod_row, norm_mix, w_in, bd, qg, kg, *, layer, j, tm, emit_cache):
    n = x.shape[0]
    tok = lambda i: (i, 0)
    const = lambda i: (0, 0)
    out_shape = [jax.ShapeDtypeStruct((n, A_WIDTH), BF)] * 3 + [jax.ShapeDtypeStruct((n, A_WIDTH), F32)]
    out_specs = [pl.BlockSpec((tm, A_WIDTH), tok)] * 4
    if emit_cache:
        out_shape += [jax.ShapeDtypeStruct((n, A_WIDTH), F32)] * 2
        out_specs += [pl.BlockSpec((tm, A_WIDTH), tok)] * 2
    return pl.pallas_call(
        functools.partial(_even_in_kernel, layer=layer, j=j),
        grid=(n // tm,),
        in_specs=[
            pl.BlockSpec((tm, D), tok),
            pl.BlockSpec((None, None, 6, D), lambda i: (layer, mod_row(i), 0, 0)),
            pl.BlockSpec(norm_mix.shape, const),
            _resident((None, D, 4 * A_WIDTH), lambda i: (j, 0, 0)),
            _resident((A_WIDTH, A_WIDTH), const),
            pl.BlockSpec(qg.shape, const),
            pl.BlockSpec(kg.shape, const),
        ],
        out_specs=out_specs,
        out_shape=out_shape,
        compiler_params=_cparams(1),
        name="even_in",
    )(x, mod4, norm_mix, w_in, bd, qg, kg)


def _dot_nt(a, b):
    return lax.dot_general(a, b, (((1,), (1,)), ((), ())), preferred_element_type=F32)


def _ctx_attn_kernel(q_ref, k_ref, v_ref, o_ref):
    lo = lax.broadcasted_iota(jnp.int32, (1, LANES), 1) < HEAD_DIM
    for pr in range(N_HEADS // 2):
        sl = slice(pr * LANES, (pr + 1) * LANES)
        q2 = q_ref[:, sl]
        k2 = k_ref[:, sl]
        v2 = v_ref[:, sl]
        outs = []
        for e in range(2):
            msk = lo if e == 0 else jnp.logical_not(lo)
            qm = jnp.where(msk, q2, jnp.zeros_like(q2))
            s = _dot_nt(qm, k2)
            m = jnp.max(s, axis=-1, keepdims=True)
            pe = jnp.exp(s - m)
            den = jnp.sum(pe, axis=-1, keepdims=True)
            outs.append(jnp.dot(pe.astype(BF), v2, preferred_element_type=F32) / den)
        o_ref[:, sl] = jnp.where(lo, outs[0], outs[1]).astype(o_ref.dtype)


def _ctx_attention(q, k, v, seq):
    n = q.shape[0]
    spec = pl.BlockSpec((seq, A_WIDTH), lambda b: (b, 0))
    return pl.pallas_call(
        _ctx_attn_kernel,
        grid=(n // seq,),
        in_specs=[spec, spec, spec],
        out_specs=spec,
        out_shape=jax.ShapeDtypeStruct((n, A_WIDTH), BF),
        compiler_params=_cparams(1),
        name="ctx_attention",
    )(q, k, v)


def _lat_attn_kernel(q_ref, k_ref, v_ref, ck_ref, cv_ref, b_ref, o_ref, *, rows):
    ws = _group_window_start(pl.program_id(1) * Q_ROWS, rows)
    k0 = pl.multiple_of(ws * GRID_W, GRID_W)
    n_loc = WIN_ROWS * GRID_W
    lo = lax.broadcasted_iota(jnp.int32, (1, LANES), 1) < HEAD_DIM
    for pr in range(N_HEADS // 2):
        sl = slice(pr * LANES, (pr + 1) * LANES)
        q2 = q_ref[:, sl]
        k2 = k_ref[pl.ds(k0, n_loc), sl]
        v2 = v_ref[pl.ds(k0, n_loc), sl]
        ck2 = ck_ref[:, sl].astype(BF)
        cv2 = cv_ref[:, sl].astype(BF)
        outs = []
        for e in range(2):
            msk = lo if e == 0 else jnp.logical_not(lo)
            qm = jnp.where(msk, q2, jnp.zeros_like(q2))
            s_loc = _dot_nt(qm, k2) + b_ref[2 * pr + e]
            s_ctx = _dot_nt(qm, ck2)
            m = jnp.maximum(jnp.max(s_loc, axis=-1, keepdims=True),
                            jnp.max(s_ctx, axis=-1, keepdims=True))
            p_loc = jnp.exp(s_loc - m)
            p_ctx = jnp.exp(s_ctx - m)
            den = jnp.sum(p_loc, axis=-1, keepdims=True) + jnp.sum(p_ctx, axis=-1, keepdims=True)
            o = (jnp.dot(p_loc.astype(BF), v2, preferred_element_type=F32)
                 + jnp.dot(p_ctx.astype(BF), cv2, preferred_element_type=F32))
            outs.append(o / den)
        o_ref[:, sl] = jnp.where(lo, outs[0], outs[1]).astype(o_ref.dtype)


def _lat_attention(q, k, v, cache_k, cache_v, bias, *, j, batch, seq):
    rows = seq // GRID_W
    groups = rows // Q_ROWS
    assert rows % Q_ROWS == 0 and groups >= 3 and rows >= WIN_ROWS
    past = cache_k.shape[2]
    tq = Q_ROWS * GRID_W

    def bias_class(g):
        return jnp.where(g == 0, 0, jnp.where(g == groups - 1, 2, 1))

    kv_spec = pl.BlockSpec((seq, A_WIDTH), lambda b, g: (b, 0))
    cache_spec = pl.BlockSpec((None, None, past, A_WIDTH), lambda b, g: (b, j, 0, 0))
    return pl.pallas_call(
        functools.partial(_lat_attn_kernel, rows=rows),
        grid=(batch, groups),
        in_specs=[
            pl.BlockSpec((tq, A_WIDTH), lambda b, g: (b * groups + g, 0)),
            kv_spec, kv_spec, cache_spec, cache_spec,
            pl.BlockSpec((None, None, N_HEADS, tq, WIN_ROWS * GRID_W),
                         lambda b, g: (j, bias_class(g), 0, 0, 0)),
        ],
        out_specs=pl.BlockSpec((tq, A_WIDTH), lambda b, g: (b * groups + g, 0)),
        out_shape=jax.ShapeDtypeStruct((batch * seq, A_WIDTH), BF),
        compiler_params=_cparams(2),
        name="lat_attention",
    )(q, k, v, cache_k, cache_v, bias)


def _even_out_kernel(x_ref, a_ref, p_ref, pp_ref, pn_ref, mod_ref, pw_ref, ps_ref, wo_ref,
                     o_ref, ext_ref, *, j, tm, tiles_per_seq, seq):
    tis = pl.program_id(0) % tiles_per_seq
    halo = pp_ref.shape[0]
    ext_ref[0:halo, :] = jnp.where(tis == 0, 0.0, pp_ref[...])
    ext_ref[halo:halo + tm, :] = p_ref[...]
    ext_ref[halo + tm:2 * halo + tm, :] = jnp.where(tis == tiles_per_seq - 1, 0.0, pn_ref[...])
    t = tis * tm + lax.broadcasted_iota(jnp.int32, (tm, 1), 0)
    pooled = []
    for g, w in enumerate(POOL_WINDOWS):
        sl = slice(g * POOL_GROUP, (g + 1) * POOL_GROUP)
        acc = None
        for off in range(-(w // 2), w // 2):
            term = ext_ref[halo + off:halo + off + tm, sl]
            acc = term if acc is None else acc + term
        cnt = (jnp.minimum(t + (w // 2 - 1), seq - 1) - jnp.maximum(t - w // 2, 0) + 1).astype(F32)
        d = acc / cnt - p_ref[:, sl]
        y = jnp.dot(d.astype(BF), pw_ref[g], preferred_element_type=F32)
        pooled.append(y * ps_ref[j:j + 1, sl])
    pooled = jnp.concatenate(pooled, axis=-1).astype(BF)
    y = (jnp.dot(a_ref[...], wo_ref[0:A_WIDTH, :], preferred_element_type=F32)
         + jnp.dot(pooled, wo_ref[A_WIDTH:, :], preferred_element_type=F32))
    o_ref[...] = x_ref[...] + mod_ref[2:3, :] * y


def _halo_specs(width, tm, n, halo):
    blocks = tm // halo
    last = n // halo - 1
    prev = pl.BlockSpec((halo, width), lambda i: (jnp.maximum(i * blocks - 1, 0), 0))
    nxt = pl.BlockSpec((halo, width), lambda i: (jnp.minimum((i + 1) * blocks, last), 0))
    return prev, nxt


def _even_out(x, attn, p, mod4, mod_row, pool_w, pool_scale, w_out, *, layer, j, tm, seq):
    n = x.shape[0]
    tok = lambda i: (i, 0)
    const = lambda i: (0, 0)
    halo = 8
    prev, nxt = _halo_specs(A_WIDTH, tm, n, halo)
    return pl.pallas_call(
        functools.partial(_even_out_kernel, j=j, tm=tm, tiles_per_seq=seq // tm, seq=seq),
        grid=(n // tm,),
        in_specs=[
            pl.BlockSpec((tm, D), tok),
            pl.BlockSpec((tm, A_WIDTH), tok),
            pl.BlockSpec((tm, A_WIDTH), tok),
            prev, nxt,
            pl.BlockSpec((None, None, 6, D), lambda i: (layer, mod_row(i), 0, 0)),
            _resident((None, len(POOL_WINDOWS), POOL_GROUP, POOL_GROUP), lambda i: (j, 0, 0, 0)),
            pl.BlockSpec(pool_scale.shape, const),
            _resident((None, D, D), lambda i: (j, 0, 0)),
        ],
        out_specs=pl.BlockSpec((tm, D), tok),
        out_shape=jax.ShapeDtypeStruct((n, D), F32),
        scratch_shapes=[pltpu.VMEM((tm + 2 * halo, A_WIDTH), F32)],
        compiler_params=_cparams(1),
        name="even_out",
    )(x, attn, p, p, p, mod4, pool_w, pool_scale, w_out)


def _conformer_kernel(x_ref, xp_ref, xn_ref, mod_ref, nmix_ref, w1_ref, dw_ref, dwb_ref,
                      lng_ref, lnb_ref, w2_ref, o_ref, h_ref, g_ref, cv_ref, y_ref,
                      *, layer, j, tm, tiles_per_seq):
    tis = pl.program_id(0) % tiles_per_seq
    g = nmix_ref[layer:layer + 1, :]
    sh = mod_ref[0:1, :]
    sc = mod_ref[1:2, :]
    x = x_ref[...]
    hp = jnp.where(tis == 0, 0.0, _modnorm(xp_ref[...], g, sh, sc))
    hn = jnp.where(tis == tiles_per_seq - 1, 0.0, _modnorm(xn_ref[...], g, sh, sc))
    h_ref[...] = jnp.concatenate([hp, _modnorm(x, g, sh, sc), hn], axis=0).astype(BF)
    halo = xp_ref.shape[0]
    seg = tm // SUBLANES
    seg_ext = seg + 2 * halo
    gp = _seg_pitch(seg_ext)
    cp = _seg_pitch(seg)
    halves = FF_CHUNK // LANES
    for c in range(D // FF_CHUNK):
        a = jnp.dot(h_ref[...], w1_ref[:, c * FF_CHUNK:(c + 1) * FF_CHUNK],
                    preferred_element_type=F32)
        gate = jnp.dot(h_ref[...], w1_ref[:, D + c * FF_CHUNK:D + (c + 1) * FF_CHUNK],
                       preferred_element_type=F32)
        glu = a * _sigmoid(gate)
        for hf in range(halves):
            for s in range(SUBLANES):
                g_ref[c * halves + hf, s * gp:s * gp + seg_ext, :] = (
                    glu[s * seg:s * seg + seg_ext, hf * LANES:(hf + 1) * LANES])

    @pl.when(pl.program_id(0) == 0)
    def _():
        cv_ref[...] = jnp.zeros_like(cv_ref)

    off = halo - CONV_K // 2
    rows_per_iter = 8
    for slab in range(D // LANES):
        sl = slice(slab * LANES, (slab + 1) * LANES)
        bias = dwb_ref[j:j + 1, sl]

        def body(ab, carry):
            for ai in range(rows_per_iter):
                a0 = ab * rows_per_iter + ai
                acc = jnp.zeros((SUBLANES, LANES), F32)
                for k in range(CONV_K):
                    acc = acc + dw_ref[k:k + 1, sl] * g_ref[slab, pl.ds(a0 + k + off, SUBLANES, stride=gp), :]
                cv_ref[slab, pl.ds(a0, SUBLANES, stride=cp), :] = acc + bias
            return carry

        lax.fori_loop(0, seg // rows_per_iter, body, 0)

    u = jnp.concatenate([cv_ref[slab] for slab in range(D // LANES)], axis=-1)
    mu = jnp.mean(u, axis=-1, keepdims=True)
    uc = u - mu
    var = jnp.mean(uc * uc, axis=-1, keepdims=True)
    y = uc * lax.rsqrt(var + EPS) * lng_ref[j:j + 1, :] + lnb_ref[j:j + 1, :]
    y = y * _sigmoid(y)
    y_ref[...] = jnp.dot(y.astype(BF), w2_ref[...], preferred_element_type=F32)
    gate_mix = mod_ref[2:3, :]
    for s in range(SUBLANES):
        rows = slice(s * seg, (s + 1) * seg)
        o_ref[rows, :] = x_ref[rows, :] + gate_mix * y_ref[s * cp:s * cp + seg, :]


def _seg_pitch(n):
    p = -(-n // 4)
    return 4 * (p if p % 2 else p + 1)


def _conformer(x, mod4, mod_row, norm_mix, w1, dw, dwb, lng, lnb, w2, *, layer, j, tm, seq):
    n = x.shape[0]
    tok = lambda i: (i, 0)
    const = lambda i: (0, 0)
    halo = 16
    prev, nxt = _halo_specs(D, tm, n, halo)
    seg = tm // SUBLANES
    gp = _seg_pitch(seg + 2 * halo)
    cp = _seg_pitch(seg)
    return pl.pallas_call(
        functools.partial(_conformer_kernel, layer=layer, j=j, tm=tm, tiles_per_seq=seq // tm),
        grid=(n // tm,),
        in_specs=[
            pl.BlockSpec((tm, D), tok), prev, nxt,
            pl.BlockSpec((None, None, 6, D), lambda i: (layer, mod_row(i), 0, 0)),
            pl.BlockSpec(norm_mix.shape, const),
            _resident((None, D, 2 * D), lambda i: (j, 0, 0)),
            pl.BlockSpec((None, CONV_K, D), lambda i: (j, 0, 0)),
            pl.BlockSpec(dwb.shape, const),
            pl.BlockSpec(lng.shape, const),
            pl.BlockSpec(lnb.shape, const),
            _resident((None, D, D), lambda i: (j, 0, 0)),
        ],
        out_specs=pl.BlockSpec((tm, D), tok),
        out_shape=jax.ShapeDtypeStruct((n, D), F32),
        scratch_shapes=[
            pltpu.VMEM((tm + 2 * halo, D), BF),
            pltpu.VMEM((D // LANES, SUBLANES * gp, LANES), F32),
            pltpu.VMEM((D // LANES, SUBLANES * cp, LANES), F32),
            pltpu.VMEM((SUBLANES * cp, D), F32),
        ],
        compiler_params=_cparams(1),
        name="conformer",
    )(x, x, x, mod4, norm_mix, w1, dw, dwb, lng, lnb, w2)


def _ffn_kernel(x_ref, xp_ref, xn_ref, mod_ref, nffn_ref, wi_ref, cw_ref, cb_ref, wo_ref,
                o_ref, h_ref, act_ref, *, layer, tm, tiles_per_seq):
    tis = pl.program_id(0) % tiles_per_seq
    g = nffn_ref[layer:layer + 1, :]
    sh = mod_ref[3:4, :]
    sc = mod_ref[4:5, :]
    x = x_ref[...]
    hp = jnp.where(tis == 0, 0.0, _modnorm(xp_ref[...], g, sh, sc))
    hn = jnp.where(tis == tiles_per_seq - 1, 0.0, _modnorm(xn_ref[...], g, sh, sc))
    z = jnp.zeros_like(hp)
    h_ref[...] = jnp.concatenate([z, hp, _modnorm(x, g, sh, sc), hn, z], axis=0).astype(BF)
    n_ext = tm + 32
    top = 16

    def conv(col):
        u = jnp.dot(h_ref[...], wi_ref[:, col:col + FF_CHUNK], preferred_element_type=F32)
        um = pltpu.roll(u, 1, 0)[top:top + tm]
        up = pltpu.roll(u, n_ext - 1, 0)[top:top + tm]
        return (cw_ref[0:1, col:col + FF_CHUNK] * um
                + cw_ref[1:2, col:col + FF_CHUNK] * u[top:top + tm]
                + cw_ref[2:3, col:col + FF_CHUNK] * up
                + cb_ref[layer:layer + 1, col:col + FF_CHUNK])

    for c in range(D_FF // FF_CHUNK):
        a = conv(c * FF_CHUNK)
        gate = conv(D_FF + c * FF_CHUNK)
        act_ref[:, c * FF_CHUNK:(c + 1) * FF_CHUNK] = (a * _sigmoid(a) * gate).astype(BF)
    y = jnp.dot(act_ref[...], wo_ref[...], preferred_element_type=F32)
    o_ref[...] = x + mod_ref[5:6, :] * y


def _ffn(x, mod4, mod_row, norm_ffn, w_in, conv_w, conv_b, w_out, *, layer, tm, seq):
    n = x.shape[0]
    tok = lambda i: (i, 0)
    const = lambda i: (0, 0)
    halo = 8
    prev, nxt = _halo_specs(D, tm, n, halo)
    return pl.pallas_call(
        functools.partial(_ffn_kernel, layer=layer, tm=tm, tiles_per_seq=seq // tm),
        grid=(n // tm,),
        in_specs=[
            pl.BlockSpec((tm, D), tok), prev, nxt,
            pl.BlockSpec((None, None, 6, D), lambda i: (layer, mod_row(i), 0, 0)),
            pl.BlockSpec(norm_ffn.shape, const),
            _resident((None, D, 2 * D_FF), lambda i: (layer, 0, 0)),
            pl.BlockSpec((None, 3, 2 * D_FF), lambda i: (layer, 0, 0)),
            pl.BlockSpec(conv_b.shape, const),
            _resident((None, D_FF, D), lambda i: (layer, 0, 0)),
        ],
        out_specs=pl.BlockSpec((tm, D), tok),
        out_shape=jax.ShapeDtypeStruct((n, D), F32),
        scratch_shapes=[
            pltpu.VMEM((tm + 32, D), BF),
            pltpu.VMEM((tm, D_FF), BF),
        ],
        compiler_params=_cparams(1),
        name="conv_ffn",
    )(x, x, x, mod4, norm_ffn, w_in, conv_w, conv_b, w_out)


def kernel(x_prompt, x_sample, cache_k, cache_v, c, c_ctx, norm_mix, norm_ffn, w_mod, b_mod,
           w_in_ab, q_gain, k_gain, rpb, pool_w, pool_scale, w_out_ab,
           conv_pw1, conv_dw, conv_dw_b, conv_ln_g, conv_ln_b, conv_pw2,
           ffn_w_in, ffn_conv_w, ffn_conv_b, ffn_w_out):
    batch, seq, _ = x_prompt.shape
    dec_batch, dec_seq, _ = x_sample.shape
    n_even = w_in_ab.shape[0]
    past = cache_k.shape[2]

    xc = x_prompt.reshape(batch * seq, D)
    xl = x_sample.reshape(dec_batch * dec_seq, D)

    ctx_row = dec_batch
    c_all = jnp.concatenate([c, c_ctx[None, :], jnp.zeros((16 - dec_batch - 1, D), F32)], axis=0)
    mod4 = _modulation(c_all, w_mod, b_mod).reshape(DEPTH, 16, 6, D)
    bias = _bias_table(rpb, dec_seq // GRID_W)

    tm_c = seq
    tm_l = 512
    tm_conf = 256
    row_c = lambda i: ctx_row
    row_l = lambda tm: (lambda i: i // (dec_seq // tm))

    head_ids = jnp.arange(A_WIDTH) // HEAD_DIM
    bd = (head_ids[:, None] == head_ids[None, :]).astype(BF)
    qg = jnp.tile(q_gain, (1, N_HEADS))
    kg = jnp.tile(k_gain, (1, N_HEADS))
    ck = cache_k.reshape(dec_batch, n_even, past, A_WIDTH)
    cv = cache_v.reshape(dec_batch, n_even, past, A_WIDTH)

    w_in_ab_b = w_in_ab.astype(BF)
    w_out_ab_b = w_out_ab.astype(BF)
    pool_w_b = pool_w.astype(BF)
    pw1_b = conv_pw1.astype(BF)
    pw2_b = conv_pw2.astype(BF)
    ffn_wi_b = ffn_w_in.astype(BF)
    ffn_wo_b = ffn_w_out.astype(BF)

    new_k, new_v = [], []
    for layer in range(DEPTH):
        j = layer // 2
        if layer % 2 == 0:
            qc, kc, vc, pc, kf, vf = _even_in(xc, mod4, row_c, norm_mix, w_in_ab_b, bd, qg, kg,
                                              layer=layer, j=j, tm=tm_c, emit_cache=True)
            ql, kl, vl, p_l = _even_in(xl, mod4, row_l(tm_l), norm_mix, w_in_ab_b, bd, qg, kg,
                                       layer=layer, j=j, tm=tm_l, emit_cache=False)
            ac = _ctx_attention(qc, kc, vc, seq)
            al = _lat_attention(ql, kl, vl, ck, cv, bias, j=j, batch=dec_batch, seq=dec_seq)
            xc = _even_out(xc, ac, pc, mod4, row_c, pool_w_b, pool_scale, w_out_ab_b,
                           layer=layer, j=j, tm=tm_c, seq=seq)
            xl = _even_out(xl, al, p_l, mod4, row_l(tm_l), pool_w_b, pool_scale, w_out_ab_b,
                           layer=layer, j=j, tm=tm_l, seq=dec_seq)
            new_k.append(kf.reshape(batch, seq, N_HEADS, HEAD_DIM))
            new_v.append(vf.reshape(batch, seq, N_HEADS, HEAD_DIM))
        else:
            xc = _conformer(xc, mod4, row_c, norm_mix, pw1_b, conv_dw, conv_dw_b, conv_ln_g,
                            conv_ln_b, pw2_b, layer=layer, j=j, tm=tm_conf, seq=seq)
            xl = _conformer(xl, mod4, row_l(tm_conf), norm_mix, pw1_b, conv_dw, conv_dw_b,
                            conv_ln_g, conv_ln_b, pw2_b, layer=layer, j=j, tm=tm_conf, seq=dec_seq)
        xc = _ffn(xc, mod4, row_c, norm_ffn, ffn_wi_b, ffn_conv_w, ffn_conv_b, ffn_wo_b,
                  layer=layer, tm=tm_c, seq=seq)
        xl = _ffn(xl, mod4, row_l(tm_l), norm_ffn, ffn_wi_b, ffn_conv_w, ffn_conv_b, ffn_wo_b,
                  layer=layer, tm=tm_l, seq=dec_seq)

    return (xc.reshape(batch, seq, D), xl.reshape(dec_batch, dec_seq, D),
            jnp.stack(new_k, axis=1), jnp.stack(new_v, axis=1))
```

```python
import functools

import jax
import jax.numpy as jnp
from jax import lax
from jax.experimental import pallas as pl
from jax.experimental.pallas import tpu as pltpu

D = 1024
DEPTH = 4
GRID_W = 64
A_WIDTH = 512
HEAD_DIM = 64
N_HEADS = 8
WIN_H = 8
WIN_W = 16
POOL_WINDOWS = (2, 4, 8, 16)
POOL_GROUP = 128
CONV_K = 31
D_FF = 2816
EPS = 1e-6
NEG_INF = -1e30

F32 = jnp.float32
BF = jnp.bfloat16

LANES = 128
SUBLANES = 8
CONV_SEGS = 16
FFN_PAD = 16
FF_CHUNK = 256
VMEM_LIMIT = 56 * 1024 * 1024


def _cparams(n_grid):
    return pltpu.CompilerParams(
        dimension_semantics=("arbitrary",) * n_grid, vmem_limit_bytes=VMEM_LIMIT)


def _resident(shape, index_map):
    return pl.BlockSpec(shape, index_map, pipeline_mode=pl.Buffered(1))


def _sigmoid(x):
    return 1.0 / (1.0 + jnp.exp(-x))


def _modnorm(x, g, shift, scale):
    ms = jnp.mean(x * x, axis=-1, keepdims=True)
    return (x * lax.rsqrt(ms + EPS) * g) * (1.0 + scale) + shift


def _mod_kernel(c_ref, w_ref, b_ref, o_ref):
    c = c_ref[...]
    s = c * _sigmoid(c)
    o_ref[...] = jnp.dot(s.astype(BF), w_ref[...].astype(BF),
                         preferred_element_type=F32) + b_ref[...]


def _modulation(c_all, w_mod, b_mod):
    tn = 1536
    return pl.pallas_call(
        _mod_kernel,
        grid=(DEPTH, 6 * D // tn),
        in_specs=[
            pl.BlockSpec((16, D), lambda l, n: (0, 0)),
            pl.BlockSpec((None, D, tn), lambda l, n: (l, 0, n)),
            pl.BlockSpec((None, 1, tn), lambda l, n: (l, 0, n)),
        ],
        out_specs=pl.BlockSpec((None, 16, tn), lambda l, n: (l, 0, n)),
        out_shape=jax.ShapeDtypeStruct((DEPTH, 16, 6 * D), F32),
        compiler_params=_cparams(2),
        name="modulation",
    )(c_all, w_mod, b_mod.reshape(DEPTH, 1, 6 * D))


Q_ROWS = WIN_H // 2
WIN_ROWS = Q_ROWS + WIN_H - 1
N_BIAS_CLASSES = 3


def _group_window_start(r0, rows):
    return jnp.clip(r0 - WIN_H // 2, 0, rows - WIN_ROWS)


def _bias_kernel(rpb_ref, o_ref, t_ref, *, rows):
    j = pl.program_id(0)
    cls = pl.program_id(1)
    q = lax.broadcasted_iota(jnp.int32, (GRID_W, LANES), 0)
    l = lax.broadcasted_iota(jnp.int32, (GRID_W, LANES), 1)
    n_rho = 2 * WIN_H - 1
    n_d = 2 * WIN_W - 1

    @pl.when(cls == 0)
    def _():
        ck = l & (GRID_W - 1)
        dc = ck - q + (WIN_W - 1)
        cs = jnp.clip(q - WIN_W // 2, 0, GRID_W - WIN_W)
        valid = (ck >= cs) & (ck < cs + WIN_W)

        def body(hr, carry):
            base = (j * (N_HEADS * n_rho) + hr) * n_d
            acc = jnp.full((GRID_W, LANES), NEG_INF, F32)
            for d in range(n_d):
                acc = jnp.where(valid & (dc == d), rpb_ref[base + d], acc)
            t_ref[hr] = acc
            return carry

        lax.fori_loop(0, N_HEADS * n_rho, body, 0)

    r0 = jnp.where(cls == 0, 0, jnp.where(cls == 1, Q_ROWS, rows - Q_ROWS))
    ws = _group_window_start(r0, rows)
    lo = l < GRID_W
    neg = jnp.full((GRID_W, LANES), NEG_INF, F32)
    for h in range(N_HEADS):
        for rq in range(Q_ROWS):
            r = r0 + rq
            rs = jnp.clip(r - WIN_H // 2, 0, rows - WIN_H)

            def blk(kappa):
                ka = ws + kappa
                ok = (ka >= rs) & (ka < rs + WIN_H)
                rho = jnp.clip(ka - r + (WIN_H - 1), 0, n_rho - 1)
                return jnp.where(ok, t_ref[h * n_rho + rho], neg)

            qs = slice(rq * GRID_W, (rq + 1) * GRID_W)
            for pr in range(WIN_ROWS // 2):
                o_ref[h, qs, pr * LANES:(pr + 1) * LANES] = jnp.where(lo, blk(2 * pr), blk(2 * pr + 1))
            if WIN_ROWS % 2:
                tail = (WIN_ROWS - 1) * GRID_W
                o_ref[h, qs, tail:tail + GRID_W] = blk(WIN_ROWS - 1)[:, 0:GRID_W]


def _bias_table(rpb, rows):
    n_even = rpb.shape[0]
    blk = (N_HEADS, Q_ROWS * GRID_W, WIN_ROWS * GRID_W)
    return pl.pallas_call(
        functools.partial(_bias_kernel, rows=rows),
        grid=(n_even, N_BIAS_CLASSES),
        in_specs=[pl.BlockSpec(memory_space=pltpu.SMEM)],
        out_specs=pl.BlockSpec((None, None) + blk, lambda j, c: (j, c, 0, 0, 0)),
        out_shape=jax.ShapeDtypeStruct((n_even, N_BIAS_CLASSES) + blk, F32),
        scratch_shapes=[pltpu.VMEM((N_HEADS * (2 * WIN_H - 1), GRID_W, LANES), F32)],
        compiler_params=_cparams(2),
        name="bias_table",
    )(rpb.reshape(-1))


def _even_in_kernel(x_ref, mod_ref, nmix_ref, w_ref, bd_ref, qg_ref, kg_ref,
                    q_ref, k_ref, v_ref, p_ref, *cache_refs, layer, j):
    g = nmix_ref[layer:layer + 1, :]
    h = _modnorm(x_ref[...], g, mod_ref[0:1, :], mod_ref[1:2, :]).astype(BF)
    proj = jnp.dot(h, w_ref[...], preferred_element_type=F32)

    def headnorm(t, gain):
        ss = jnp.dot((t * t).astype(BF), bd_ref[...], preferred_element_type=F32)
        return t * lax.rsqrt(ss * (1.0 / HEAD_DIM) + EPS) * gain

    q = headnorm(proj[:, 0:A_WIDTH], qg_ref[j:j + 1, :])
    k = headnorm(proj[:, A_WIDTH:2 * A_WIDTH], kg_ref[j:j + 1, :])
    v = proj[:, 2 * A_WIDTH:3 * A_WIDTH]
    q_ref[...] = (q * (HEAD_DIM ** -0.5)).astype(BF)
    k_ref[...] = k.astype(BF)
    v_ref[...] = v.astype(BF)
    p_ref[...] = proj[:, 3 * A_WIDTH:]
    if cache_refs:
        cache_refs[0][...] = k
        cache_refs[1][...] = v


def _even_in(x, mod4, mod_row, norm_mix, w_in, bd, qg, kg, *, layer, j, tm, emit_cache):
    n = x.shape[0]
    tok = lambda i: (i, 0)
    const = lambda i: (0, 0)
    out_shape = [jax.ShapeDtypeStruct((n, A_WIDTH), BF)] * 3 + [jax.ShapeDtypeStruct((n, A_WIDTH), F32)]
    out_specs = [pl.BlockSpec((tm, A_WIDTH), tok)] * 4
    if emit_cache:
        out_shape += [jax.ShapeDtypeStruct((n, A_WIDTH), F32)] * 2
        out_specs += [pl.BlockSpec((tm, A_WIDTH), tok)] * 2
    return pl.pallas_call(
        functools.partial(_even_in_kernel, layer=layer, j=j),
        grid=(n // tm,),
        in_specs=[
            pl.BlockSpec((tm, D), tok),
            pl.BlockSpec((None, None, 6, D), lambda i: (layer, mod_row(i), 0, 0)),
            pl.BlockSpec(norm_mix.shape, const),
            _resident((None, D, 4 * A_WIDTH), lambda i: (j, 0, 0)),
            _resident((A_WIDTH, A_WIDTH), const),
            pl.BlockSpec(qg.shape, const),
            pl.BlockSpec(kg.shape, const),
        ],
        out_specs=out_specs,
        out_shape=out_shape,
        compiler_params=_cparams(1),
        name="even_in",
    )(x, mod4, norm_mix, w_in, bd, qg, kg)


def _dot_nt(a, b):
    return lax.dot_general(a, b, (((1,), (1,)), ((), ())), preferred_element_type=F32)


def _ctx_attn_kernel(q_ref, k_ref, v_ref, o_ref):
    tq = q_ref.shape[0]
    lo = lax.broadcasted_iota(jnp.int32, (1, LANES), 1) < HEAD_DIM
    for pr in range(N_HEADS // 2):
        sl = slice(pr * LANES, (pr + 1) * LANES)
        q2 = q_ref[:, sl]
        k2 = k_ref[:, sl]
        v2 = v_ref[:, sl]
        zero = jnp.zeros_like(q2)
        qm = jnp.concatenate([jnp.where(lo, q2, zero), jnp.where(lo, zero, q2)], axis=0)
        s = _dot_nt(qm, k2)
        m = jnp.max(s, axis=-1, keepdims=True)
        pe = jnp.exp(s - m)
        den = jnp.sum(pe, axis=-1, keepdims=True)
        o = jnp.dot(pe.astype(BF), v2, preferred_element_type=F32) / den
        o_ref[:, sl] = jnp.where(lo, o[0:tq], o[tq:2 * tq]).astype(o_ref.dtype)


def _ctx_attention(q, k, v, seq):
    n = q.shape[0]
    spec = pl.BlockSpec((seq, A_WIDTH), lambda b: (b, 0))
    return pl.pallas_call(
        _ctx_attn_kernel,
        grid=(n // seq,),
        in_specs=[spec, spec, spec],
        out_specs=spec,
        out_shape=jax.ShapeDtypeStruct((n, A_WIDTH), BF),
        compiler_params=_cparams(1),
        name="ctx_attention",
    )(q, k, v)


def _lat_attn_kernel(q_ref, k_ref, v_ref, ck_ref, cv_ref, b_ref, o_ref, *, rows):
    ws = _group_window_start(pl.program_id(1) * Q_ROWS, rows)
    k0 = pl.multiple_of(ws * GRID_W, GRID_W)
    n_loc = WIN_ROWS * GRID_W
    tq = q_ref.shape[0]
    lo = lax.broadcasted_iota(jnp.int32, (1, LANES), 1) < HEAD_DIM
    for pr in range(N_HEADS // 2):
        sl = slice(pr * LANES, (pr + 1) * LANES)
        q2 = q_ref[:, sl]
        k2 = k_ref[pl.ds(k0, n_loc), sl]
        v2 = v_ref[pl.ds(k0, n_loc), sl]
        ck2 = ck_ref[:, sl].astype(BF)
        cv2 = cv_ref[:, sl].astype(BF)
        zero = jnp.zeros_like(q2)
        qm = jnp.concatenate([jnp.where(lo, q2, zero), jnp.where(lo, zero, q2)], axis=0)
        s_loc = _dot_nt(qm, k2) + b_ref[2 * pr:2 * pr + 2].reshape(2 * tq, n_loc)
        s_ctx = _dot_nt(qm, ck2)
        m = jnp.maximum(jnp.max(s_loc, axis=-1, keepdims=True),
                        jnp.max(s_ctx, axis=-1, keepdims=True))
        p_loc = jnp.exp(s_loc - m)
        p_ctx = jnp.exp(s_ctx - m)
        den = jnp.sum(p_loc, axis=-1, keepdims=True) + jnp.sum(p_ctx, axis=-1, keepdims=True)
        o = (jnp.dot(p_loc.astype(BF), v2, preferred_element_type=F32)
             + jnp.dot(p_ctx.astype(BF), cv2, preferred_element_type=F32)) / den
        o_ref[:, sl] = jnp.where(lo, o[0:tq], o[tq:2 * tq]).astype(o_ref.dtype)


def _lat_attention(q, k, v, cache_k, cache_v, bias, *, j, batch, seq):
    rows = seq // GRID_W
    groups = rows // Q_ROWS
    assert rows % Q_ROWS == 0 and groups >= 3 and rows >= WIN_ROWS
    past = cache_k.shape[2]
    tq = Q_ROWS * GRID_W

    def bias_class(g):
        return jnp.where(g == 0, 0, jnp.where(g == groups - 1, 2, 1))

    kv_spec = pl.BlockSpec((seq, A_WIDTH), lambda b, g: (b, 0))
    cache_spec = pl.BlockSpec((None, None, past, A_WIDTH), lambda b, g: (b, j, 0, 0))
    return pl.pallas_call(
        functools.partial(_lat_attn_kernel, rows=rows),
        grid=(batch, groups),
        in_specs=[
            pl.BlockSpec((tq, A_WIDTH), lambda b, g: (b * groups + g, 0)),
            kv_spec, kv_spec, cache_spec, cache_spec,
            pl.BlockSpec((None, None, N_HEADS, tq, WIN_ROWS * GRID_W),
                         lambda b, g: (j, bias_class(g), 0, 0, 0)),
        ],
        out_specs=pl.BlockSpec((tq, A_WIDTH), lambda b, g: (b * groups + g, 0)),
        out_shape=jax.ShapeDtypeStruct((batch * seq, A_WIDTH), BF),
        compiler_params=_cparams(2),
        name="lat_attention",
    )(q, k, v, cache_k, cache_v, bias)


def _even_out_kernel(x_ref, a_ref, p_ref, pp_ref, pn_ref, mod_ref, pw_ref, ps_ref, wo_ref,
                     o_ref, ext_ref, *, j, tm, tiles_per_seq, seq):
    tis = pl.program_id(0) % tiles_per_seq
    halo = pp_ref.shape[0]
    ext_ref[0:halo, :] = jnp.where(tis == 0, 0.0, pp_ref[...])
    ext_ref[halo:halo + tm, :] = p_ref[...]
    ext_ref[halo + tm:2 * halo + tm, :] = jnp.where(tis == tiles_per_seq - 1, 0.0, pn_ref[...])
    t = tis * tm + lax.broadcasted_iota(jnp.int32, (tm, 1), 0)
    pooled = []
    for g, w in enumerate(POOL_WINDOWS):
        sl = slice(g * POOL_GROUP, (g + 1) * POOL_GROUP)
        acc = None
        for off in range(-(w // 2), w // 2):
            term = ext_ref[halo + off:halo + off + tm, sl]
            acc = term if acc is None else acc + term
        cnt = (jnp.minimum(t + (w // 2 - 1), seq - 1) - jnp.maximum(t - w // 2, 0) + 1).astype(F32)
        d = acc / cnt - p_ref[:, sl]
        y = jnp.dot(d.astype(BF), pw_ref[g], preferred_element_type=F32)
        pooled.append(y * ps_ref[j:j + 1, sl])
    pooled = jnp.concatenate(pooled, axis=-1).astype(BF)
    y = (jnp.dot(a_ref[...], wo_ref[0:A_WIDTH, :], preferred_element_type=F32)
         + jnp.dot(pooled, wo_ref[A_WIDTH:, :], preferred_element_type=F32))
    o_ref[...] = x_ref[...] + mod_ref[2:3, :] * y


def _halo_specs(width, tm, n, halo):
    blocks = tm // halo
    last = n // halo - 1
    prev = pl.BlockSpec((halo, width), lambda i: (jnp.maximum(i * blocks - 1, 0), 0))
    nxt = pl.BlockSpec((halo, width), lambda i: (jnp.minimum((i + 1) * blocks, last), 0))
    return prev, nxt


def _even_out(x, attn, p, mod4, mod_row, pool_w, pool_scale, w_out, *, layer, j, tm, seq):
    n = x.shape[0]
    tok = lambda i: (i, 0)
    const = lambda i: (0, 0)
    halo = 8
    prev, nxt = _halo_specs(A_WIDTH, tm, n, halo)
    return pl.pallas_call(
        functools.partial(_even_out_kernel, j=j, tm=tm, tiles_per_seq=seq // tm, seq=seq),
        grid=(n // tm,),
        in_specs=[
            pl.BlockSpec((tm, D), tok),
            pl.BlockSpec((tm, A_WIDTH), tok),
            pl.BlockSpec((tm, A_WIDTH), tok),
            prev, nxt,
            pl.BlockSpec((None, None, 6, D), lambda i: (layer, mod_row(i), 0, 0)),
            _resident((None, len(POOL_WINDOWS), POOL_GROUP, POOL_GROUP), lambda i: (j, 0, 0, 0)),
            pl.BlockSpec(pool_scale.shape, const),
            _resident((None, D, D), lambda i: (j, 0, 0)),
        ],
        out_specs=pl.BlockSpec((tm, D), tok),
        out_shape=jax.ShapeDtypeStruct((n, D), F32),
        scratch_shapes=[pltpu.VMEM((tm + 2 * halo, A_WIDTH), F32)],
        compiler_params=_cparams(1),
        name="even_out",
    )(x, attn, p, p, p, mod4, pool_w, pool_scale, w_out)


def _conf_glu_kernel(x_ref, xp_ref, xn_ref, mod_ref, nmix_ref, w1_ref, gb_ref, h_ref, g_ref,
                     *, layer, tm, nsub, tiles_per_seq):
    g = nmix_ref[layer:layer + 1, :]
    sh = mod_ref[0:1, :]
    sc = mod_ref[1:2, :]
    halo = xp_ref.shape[0]
    sub_len = tm // nsub
    hm = _modnorm(x_ref[...], g, sh, sc)
    if nsub == 1:
        tis = pl.program_id(0) % tiles_per_seq
        hp = jnp.where(tis == 0, 0.0, _modnorm(xp_ref[...], g, sh, sc))
        hn = jnp.where(tis == tiles_per_seq - 1, 0.0, _modnorm(xn_ref[...], g, sh, sc))
        pieces = [hp, hm, hn]
    else:
        z = jnp.zeros((halo, D), F32)
        pieces = [z]
        for q in range(nsub):
            pieces += [hm[q * sub_len:(q + 1) * sub_len], z]
    h_ref[...] = jnp.concatenate(pieces, axis=0).astype(BF)

    seg = tm // CONV_SEGS
    seg_ext = seg + 2 * halo
    gp = _seg_pitch(seg_ext)
    segs_per_sub = CONV_SEGS // nsub
    halves = FF_CHUNK // LANES
    half_segs = CONV_SEGS // 2
    for c in range(D // FF_CHUNK):
        a = jnp.dot(h_ref[...], w1_ref[:, c * FF_CHUNK:(c + 1) * FF_CHUNK],
                    preferred_element_type=F32)
        gate = jnp.dot(h_ref[...], w1_ref[:, D + c * FF_CHUNK:D + (c + 1) * FF_CHUNK],
                       preferred_element_type=F32)
        glu = a * _sigmoid(gate)
        for hf in range(halves):
            slab = c * halves + hf
            for s in range(CONV_SEGS):
                start = s * seg + halo * (s // segs_per_sub)
                g_ref[slab, s * gp:s * gp + seg_ext, :] = (
                    glu[start:start + seg_ext, hf * LANES:(hf + 1) * LANES])
            for a0 in range(seg_ext):
                lo = g_ref[slab, pl.ds(a0, SUBLANES, stride=gp), :]
                hi = g_ref[slab, pl.ds(half_segs * gp + a0, SUBLANES, stride=gp), :]
                gb_ref[slab, a0 * CONV_SEGS:(a0 + 1) * CONV_SEGS, :] = (
                    jnp.concatenate([lo, hi], axis=0).astype(BF))


def _conf_conv_kernel(gb_ref, dw_ref, dwb_ref, u_ref, wb_ref, *, j, seg, off):
    @pl.when(pl.program_id(0) == 0)
    def _():
        for k in range(CONV_K):
            wk = jnp.broadcast_to(dw_ref[k:k + 1, :], (CONV_SEGS, D)).astype(BF)
            for slab in range(D // LANES):
                wb_ref[slab, k * CONV_SEGS:(k + 1) * CONV_SEGS, :] = wk[:, slab * LANES:(slab + 1) * LANES]

    for slab in range(D // LANES):
        bias = dwb_ref[j:j + 1, slab * LANES:(slab + 1) * LANES]
        for a0 in range(seg):
            acc = jnp.zeros((CONV_SEGS, LANES), F32)
            for k in range(CONV_K):
                t0 = (a0 + k + off) * CONV_SEGS
                acc = acc + (gb_ref[slab, t0:t0 + CONV_SEGS, :].astype(F32)
                             * wb_ref[slab, k * CONV_SEGS:(k + 1) * CONV_SEGS, :].astype(F32))
            u_ref[slab, a0 * CONV_SEGS:(a0 + 1) * CONV_SEGS, :] = acc + bias


def _conf_out_kernel(u_ref, x_ref, mod_ref, lng_ref, lnb_ref, w2_ref, o_ref, y_ref, *, j, seg):
    n_slabs = D // LANES
    u = jnp.concatenate([u_ref[slab] for slab in range(n_slabs)], axis=-1)
    mu = jnp.mean(u, axis=-1, keepdims=True)
    uc = u - mu
    var = jnp.mean(uc * uc, axis=-1, keepdims=True)
    y = uc * lax.rsqrt(var + EPS) * lng_ref[j:j + 1, :] + lnb_ref[j:j + 1, :]
    y = y * _sigmoid(y)
    y = jnp.dot(y.astype(BF), w2_ref[...], preferred_element_type=F32)
    for slab in range(n_slabs):
        y_ref[slab] = y[:, slab * LANES:(slab + 1) * LANES]
    gate_mix = mod_ref[2:3, :]
    for s in range(CONV_SEGS):
        ys = jnp.concatenate(
            [y_ref[slab, pl.ds(s, seg, stride=CONV_SEGS), :] for slab in range(n_slabs)], axis=-1)
        rows = slice(s * seg, (s + 1) * seg)
        o_ref[rows, :] = x_ref[rows, :] + gate_mix * ys


def _seg_pitch(n):
    p = -(-n // 4)
    return 4 * (p if p % 2 else p + 1)


def _conformer(x, mod4, mod_row, norm_mix, w1, dw, dwb, lng, lnb, w2, *, layer, j, tm, seq):
    n = x.shape[0]
    tok = lambda i: (i, 0)
    const = lambda i: (0, 0)
    halo = 16
    prev, nxt = _halo_specs(D, tm, n, halo)
    nsub = max(tm // seq, 1)
    assert tm % seq == 0 or seq % tm == 0
    assert CONV_SEGS == 2 * SUBLANES and CONV_SEGS % nsub == 0 and halo >= CONV_K // 2
    seg = tm // CONV_SEGS
    seg_ext = seg + 2 * halo
    gp = _seg_pitch(seg_ext)
    ext_rows = tm + halo * (nsub + 1)
    n_tiles = n // tm
    n_slabs = D // LANES
    mod_spec = pl.BlockSpec((None, None, 6, D), lambda i: (layer, mod_row(i), 0, 0))
    gb_spec = pl.BlockSpec((None, n_slabs, seg_ext * CONV_SEGS, LANES), lambda i: (i, 0, 0, 0))
    u_spec = pl.BlockSpec((None, n_slabs, tm, LANES), lambda i: (i, 0, 0, 0))

    gb = pl.pallas_call(
        functools.partial(_conf_glu_kernel, layer=layer, tm=tm, nsub=nsub,
                          tiles_per_seq=max(seq // tm, 1)),
        grid=(n_tiles,),
        in_specs=[
            pl.BlockSpec((tm, D), tok), prev, nxt, mod_spec,
            pl.BlockSpec(norm_mix.shape, const),
            _resident((None, D, 2 * D), lambda i: (j, 0, 0)),
        ],
        out_specs=gb_spec,
        out_shape=jax.ShapeDtypeStruct((n_tiles, n_slabs, seg_ext * CONV_SEGS, LANES), BF),
        scratch_shapes=[
            pltpu.VMEM((ext_rows, D), BF),
            pltpu.VMEM((n_slabs, CONV_SEGS * gp, LANES), F32),
        ],
        compiler_params=_cparams(1),
        name="conf_glu",
    )(x, x, x, mod4, norm_mix, w1)

    u = pl.pallas_call(
        functools.partial(_conf_conv_kernel, j=j, seg=seg, off=halo - CONV_K // 2),
        grid=(n_tiles,),
        in_specs=[
            gb_spec,
            pl.BlockSpec((None, CONV_K, D), lambda i: (j, 0, 0)),
            pl.BlockSpec(dwb.shape, const),
        ],
        out_specs=u_spec,
        out_shape=jax.ShapeDtypeStruct((n_tiles, n_slabs, tm, LANES), F32),
        scratch_shapes=[pltpu.VMEM((n_slabs, CONV_K * CONV_SEGS, LANES), BF)],
        compiler_params=_cparams(1),
        name="conf_conv",
    )(gb, dw, dwb)

    return pl.pallas_call(
        functools.partial(_conf_out_kernel, j=j, seg=seg),
        grid=(n_tiles,),
        in_specs=[
            u_spec,
            pl.BlockSpec((tm, D), tok), mod_spec,
            pl.BlockSpec(lng.shape, const),
            pl.BlockSpec(lnb.shape, const),
            _resident((None, D, D), lambda i: (j, 0, 0)),
        ],
        out_specs=pl.BlockSpec((tm, D), tok),
        out_shape=jax.ShapeDtypeStruct((n, D), F32),
        scratch_shapes=[pltpu.VMEM((n_slabs, tm, LANES), F32)],
        compiler_params=_cparams(1),
        name="conf_out",
    )(u, x, mod4, lng, lnb, w2)


def _ffn_kernel(x_ref, xp_ref, xn_ref, mod_ref, nffn_ref, wi_ref, cw_ref, cb_ref, wo_ref,
                o_ref, h_ref, act_ref, *, layer, tm, nsub, tiles_per_seq):
    g = nffn_ref[layer:layer + 1, :]
    sh = mod_ref[3:4, :]
    sc = mod_ref[4:5, :]
    halo = xp_ref.shape[0]
    sub_len = tm // nsub
    hm = _modnorm(x_ref[...], g, sh, sc)
    if nsub == 1:
        tis = pl.program_id(0) % tiles_per_seq
        hp = jnp.where(tis == 0, 0.0, _modnorm(xp_ref[...], g, sh, sc))
        hn = jnp.where(tis == tiles_per_seq - 1, 0.0, _modnorm(xn_ref[...], g, sh, sc))
        z = jnp.zeros((FFN_PAD - halo, D), F32)
        pieces = [z, hp, hm, hn, z]
    else:
        z = jnp.zeros((FFN_PAD, D), F32)
        pieces = [z]
        for q in range(nsub):
            pieces += [hm[q * sub_len:(q + 1) * sub_len], z]
    h_ref[...] = jnp.concatenate(pieces, axis=0).astype(BF)
    n_ext = tm + FFN_PAD * (nsub + 1)

    def conv(col):
        u = jnp.dot(h_ref[...], wi_ref[:, col:col + FF_CHUNK], preferred_element_type=F32)
        um = pltpu.roll(u, 1, 0)
        up = pltpu.roll(u, n_ext - 1, 0)
        return (cw_ref[0:1, col:col + FF_CHUNK] * um
                + cw_ref[1:2, col:col + FF_CHUNK] * u
                + cw_ref[2:3, col:col + FF_CHUNK] * up
                + cb_ref[layer:layer + 1, col:col + FF_CHUNK])

    for c in range(D_FF // FF_CHUNK):
        a = conv(c * FF_CHUNK)
        gate = conv(D_FF + c * FF_CHUNK)
        act = (a * _sigmoid(a) * gate).astype(BF)
        for q in range(nsub):
            r0 = FFN_PAD + q * (sub_len + FFN_PAD)
            act_ref[q * sub_len:(q + 1) * sub_len, c * FF_CHUNK:(c + 1) * FF_CHUNK] = act[r0:r0 + sub_len]
    y = jnp.dot(act_ref[...], wo_ref[...], preferred_element_type=F32)
    o_ref[...] = x_ref[...] + mod_ref[5:6, :] * y


def _ffn(x, mod4, mod_row, norm_ffn, w_in, conv_w, conv_b, w_out, *, layer, tm, seq):
    n = x.shape[0]
    tok = lambda i: (i, 0)
    const = lambda i: (0, 0)
    halo = 8
    prev, nxt = _halo_specs(D, tm, n, halo)
    nsub = max(tm // seq, 1)
    assert tm % seq == 0 or seq % tm == 0
    return pl.pallas_call(
        functools.partial(_ffn_kernel, layer=layer, tm=tm, nsub=nsub,
                          tiles_per_seq=max(seq // tm, 1)),
        grid=(n // tm,),
        in_specs=[
            pl.BlockSpec((tm, D), tok), prev, nxt,
            pl.BlockSpec((None, None, 6, D), lambda i: (layer, mod_row(i), 0, 0)),
            pl.BlockSpec(norm_ffn.shape, const),
            _resident((None, D, 2 * D_FF), lambda i: (layer, 0, 0)),
            pl.BlockSpec((None, 3, 2 * D_FF), lambda i: (layer, 0, 0)),
            pl.BlockSpec(conv_b.shape, const),
            _resident((None, D_FF, D), lambda i: (layer, 0, 0)),
        ],
        out_specs=pl.BlockSpec((tm, D), tok),
        out_shape=jax.ShapeDtypeStruct((n, D), F32),
        scratch_shapes=[
            pltpu.VMEM((tm + FFN_PAD * (nsub + 1), D), BF),
            pltpu.VMEM((tm, D_FF), BF),
        ],
        compiler_params=_cparams(1),
        name="conv_ffn",
    )(x, x, x, mod4, norm_ffn, w_in, conv_w, conv_b, w_out)


def kernel(x_prompt, x_sample, cache_k, cache_v, c, c_ctx, norm_mix, norm_ffn, w_mod, b_mod,
           w_in_ab, q_gain, k_gain, rpb, pool_w, pool_scale, w_out_ab,
           conv_pw1, conv_dw, conv_dw_b, conv_ln_g, conv_ln_b, conv_pw2,
           ffn_w_in, ffn_conv_w, ffn_conv_b, ffn_w_out):
    batch, seq, _ = x_prompt.shape
    dec_batch, dec_seq, _ = x_sample.shape
    n_even = w_in_ab.shape[0]
    past = cache_k.shape[2]

    xc = x_prompt.reshape(batch * seq, D)
    xl = x_sample.reshape(dec_batch * dec_seq, D)

    ctx_row = dec_batch
    c_all = jnp.concatenate([c, c_ctx[None, :], jnp.zeros((16 - dec_batch - 1, D), F32)], axis=0)
    mod4 = _modulation(c_all, w_mod, b_mod).reshape(DEPTH, 16, 6, D)
    bias = _bias_table(rpb, dec_seq // GRID_W)

    tm_c = seq
    tm_l = 512
    tm_conf = 512
    tm_ffn = 1024
    row_c = lambda i: ctx_row
    row_l = lambda tm: (lambda i: i // (dec_seq // tm))

    head_ids = jnp.arange(A_WIDTH) // HEAD_DIM
    bd = (head_ids[:, None] == head_ids[None, :]).astype(BF)
    qg = jnp.tile(q_gain, (1, N_HEADS))
    kg = jnp.tile(k_gain, (1, N_HEADS))
    ck = cache_k.reshape(dec_batch, n_even, past, A_WIDTH)
    cv = cache_v.reshape(dec_batch, n_even, past, A_WIDTH)

    w_in_ab_b = w_in_ab.astype(BF)
    w_out_ab_b = w_out_ab.astype(BF)
    pool_w_b = pool_w.astype(BF)
    pw1_b = conv_pw1.astype(BF)
    pw2_b = conv_pw2.astype(BF)
    ffn_wi_b = ffn_w_in.astype(BF)
    ffn_wo_b = ffn_w_out.astype(BF)

    new_k, new_v = [], []
    for layer in range(DEPTH):
        j = layer // 2
        if layer % 2 == 0:
            qc, kc, vc, pc, kf, vf = _even_in(xc, mod4, row_c, norm_mix, w_in_ab_b, bd, qg, kg,
                                              layer=layer, j=j, tm=tm_c, emit_cache=True)
            ql, kl, vl, p_l = _even_in(xl, mod4, row_l(tm_l), norm_mix, w_in_ab_b, bd, qg, kg,
                                       layer=layer, j=j, tm=tm_l, emit_cache=False)
            ac = _ctx_attention(qc, kc, vc, seq)
            al = _lat_attention(ql, kl, vl, ck, cv, bias, j=j, batch=dec_batch, seq=dec_seq)
            xc = _even_out(xc, ac, pc, mod4, row_c, pool_w_b, pool_scale, w_out_ab_b,
                           layer=layer, j=j, tm=tm_c, seq=seq)
            xl = _even_out(xl, al, p_l, mod4, row_l(tm_l), pool_w_b, pool_scale, w_out_ab_b,
                           layer=layer, j=j, tm=tm_l, seq=dec_seq)
            new_k.append(kf.reshape(batch, seq, N_HEADS, HEAD_DIM))
            new_v.append(vf.reshape(batch, seq, N_HEADS, HEAD_DIM))
        else:
            xc = _conformer(xc, mod4, row_c, norm_mix, pw1_b, conv_dw, conv_dw_b, conv_ln_g,
                            conv_ln_b, pw2_b, layer=layer, j=j, tm=tm_conf, seq=seq)
            xl = _conformer(xl, mod4, row_l(tm_conf), norm_mix, pw1_b, conv_dw, conv_dw_b,
                            conv_ln_g, conv_ln_b, pw2_b, layer=layer, j=j, tm=tm_conf, seq=dec_seq)
        xc = _ffn(xc, mod4, row_c, norm_ffn, ffn_wi_b, ffn_conv_w, ffn_conv_b, ffn_wo_b,
                  layer=layer, tm=tm_ffn, seq=seq)
        xl = _ffn(xl, mod4, row_l(tm_ffn), norm_ffn, ffn_wi_b, ffn_conv_w, ffn_conv_b, ffn_wo_b,
                  layer=layer, tm=tm_ffn, seq=dec_seq)

    return (xc.reshape(batch, seq, D), xl.reshape(dec_batch, dec_seq, D),
            jnp.stack(new_k, axis=1), jnp.stack(new_v, axis=1))
```

```python
import functools

import jax
import jax.numpy as jnp
from jax import lax
from jax.experimental import pallas as pl
from jax.experimental.pallas import tpu as pltpu

D = 1024
DEPTH = 4
GRID_W = 64
A_WIDTH = 512
HEAD_DIM = 64
N_HEADS = 8
WIN_H = 8
WIN_W = 16
POOL_WINDOWS = (2, 4, 8, 16)
POOL_GROUP = 128
CONV_K = 31
D_FF = 2816
EPS = 1e-6
NEG_INF = -1e30

F32 = jnp.float32
BF = jnp.bfloat16

LANES = 128
SUBLANES = 8
CONV_SEGS = 16
FFN_PAD = 16
FF_CHUNK = 256
VMEM_LIMIT = 56 * 1024 * 1024


def _cparams(n_grid):
    return pltpu.CompilerParams(
        dimension_semantics=("arbitrary",) * n_grid, vmem_limit_bytes=VMEM_LIMIT)


def _resident(shape, index_map):
    return pl.BlockSpec(shape, index_map, pipeline_mode=pl.Buffered(1))


def _sigmoid(x):
    return 1.0 / (1.0 + jnp.exp(-x))


def _modnorm(x, g, shift, scale):
    ms = jnp.mean(x * x, axis=-1, keepdims=True)
    return (x * lax.rsqrt(ms + EPS) * g) * (1.0 + scale) + shift


def _mod_kernel(c_ref, w_ref, b_ref, o_ref):
    c = c_ref[...]
    s = c * _sigmoid(c)
    o_ref[...] = jnp.dot(s.astype(BF), w_ref[...].astype(BF),
                         preferred_element_type=F32) + b_ref[...]


def _modulation(c_all, w_mod, b_mod):
    tn = 1536
    return pl.pallas_call(
        _mod_kernel,
        grid=(DEPTH, 6 * D // tn),
        in_specs=[
            pl.BlockSpec((16, D), lambda l, n: (0, 0)),
            pl.BlockSpec((None, D, tn), lambda l, n: (l, 0, n)),
            pl.BlockSpec((None, 1, tn), lambda l, n: (l, 0, n)),
        ],
        out_specs=pl.BlockSpec((None, 16, tn), lambda l, n: (l, 0, n)),
        out_shape=jax.ShapeDtypeStruct((DEPTH, 16, 6 * D), F32),
        compiler_params=_cparams(2),
        name="modulation",
    )(c_all, w_mod, b_mod.reshape(DEPTH, 1, 6 * D))


Q_ROWS = WIN_H // 2
WIN_ROWS = Q_ROWS + WIN_H - 1
N_BIAS_CLASSES = 3


def _group_window_start(r0, rows):
    return jnp.clip(r0 - WIN_H // 2, 0, rows - WIN_ROWS)


def _bias_kernel(b_ref, o_ref, t_ref, *, rows):
    cls = pl.program_id(1)
    q = lax.broadcasted_iota(jnp.int32, (GRID_W, LANES), 0)
    l = lax.broadcasted_iota(jnp.int32, (GRID_W, LANES), 1)
    n_rho = 2 * WIN_H - 1
    lo = l < GRID_W

    @pl.when(cls == 0)
    def _():
        ck = l & (GRID_W - 1)
        cs = jnp.clip(q - WIN_W // 2, 0, GRID_W - WIN_W)
        valid = (ck >= cs) & (ck < cs + WIN_W)

        def body(hr, carry):
            row = jnp.broadcast_to(b_ref[pl.ds(hr, 1), :], (GRID_W, LANES))
            r_lo = pltpu.roll(row, 0, 1, stride=1, stride_axis=0)
            r_hi = pltpu.roll(row, GRID_W, 1, stride=1, stride_axis=0)
            t_ref[hr] = jnp.where(valid, jnp.where(lo, r_lo, r_hi), NEG_INF)
            return carry

        lax.fori_loop(0, N_HEADS * n_rho, body, 0, unroll=8)

    r0 = jnp.where(cls == 0, 0, jnp.where(cls == 1, Q_ROWS, rows - Q_ROWS))
    ws = _group_window_start(r0, rows)
    neg = jnp.full((GRID_W, LANES), NEG_INF, F32)
    for h in range(N_HEADS):
        for rq in range(Q_ROWS):
            r = r0 + rq
            rs = jnp.clip(r - WIN_H // 2, 0, rows - WIN_H)

            def blk(kappa):
                ka = ws + kappa
                ok = (ka >= rs) & (ka < rs + WIN_H)
                rho = jnp.clip(ka - r + (WIN_H - 1), 0, n_rho - 1)
                return jnp.where(ok, t_ref[h * n_rho + rho], neg)

            qs = slice(rq * GRID_W, (rq + 1) * GRID_W)
            for pr in range(WIN_ROWS // 2):
                o_ref[h, qs, pr * LANES:(pr + 1) * LANES] = jnp.where(lo, blk(2 * pr), blk(2 * pr + 1))
            if WIN_ROWS % 2:
                tail = (WIN_ROWS - 1) * GRID_W
                o_ref[h, qs, tail:tail + GRID_W] = blk(WIN_ROWS - 1)[:, 0:GRID_W]


def _bias_table(rpb, rows):
    n_even = rpb.shape[0]
    blk = (N_HEADS, Q_ROWS * GRID_W, WIN_ROWS * GRID_W)
    n_hr = N_HEADS * (2 * WIN_H - 1)
    n_d = 2 * WIN_W - 1
    rp = rpb.reshape(n_even, n_hr, n_d)
    b_rows = jnp.concatenate(
        [rp[..., WIN_W - 1:], jnp.zeros((n_even, n_hr, LANES - n_d), F32), rp[..., :WIN_W - 1]], axis=-1)
    return pl.pallas_call(
        functools.partial(_bias_kernel, rows=rows),
        grid=(n_even, N_BIAS_CLASSES),
        in_specs=[pl.BlockSpec((None, n_hr, LANES), lambda j, c: (j, 0, 0))],
        out_specs=pl.BlockSpec((None, None) + blk, lambda j, c: (j, c, 0, 0, 0)),
        out_shape=jax.ShapeDtypeStruct((n_even, N_BIAS_CLASSES) + blk, F32),
        scratch_shapes=[pltpu.VMEM((N_HEADS * (2 * WIN_H - 1), GRID_W, LANES), F32)],
        compiler_params=_cparams(2),
        name="bias_table",
    )(b_rows)


def _even_in_kernel(x_ref, mod_ref, nmix_ref, w_ref, bd_ref, qg_ref, kg_ref,
                    q_ref, k_ref, v_ref, p_ref, *cache_refs, layer, j):
    g = nmix_ref[layer:layer + 1, :]
    h = _modnorm(x_ref[...], g, mod_ref[0:1, :], mod_ref[1:2, :]).astype(BF)
    proj = jnp.dot(h, w_ref[...], preferred_element_type=F32)

    def headnorm(t, gain):
        ss = jnp.dot((t * t).astype(BF), bd_ref[...], preferred_element_type=F32)
        return t * lax.rsqrt(ss * (1.0 / HEAD_DIM) + EPS) * gain

    q = headnorm(proj[:, 0:A_WIDTH], qg_ref[j:j + 1, :])
    k = headnorm(proj[:, A_WIDTH:2 * A_WIDTH], kg_ref[j:j + 1, :])
    v = proj[:, 2 * A_WIDTH:3 * A_WIDTH]
    q_ref[...] = (q * (HEAD_DIM ** -0.5)).astype(BF)
    k_ref[...] = k.astype(BF)
    v_ref[...] = v.astype(BF)
    p_ref[...] = proj[:, 3 * A_WIDTH:]
    if cache_refs:
        cache_refs[0][...] = k
        cache_refs[1][...] = v


def _even_in(x, mod4, mod_row, norm_mix, w_in, bd, qg, kg, *, layer, j, tm, emit_cache):
    n = x.shape[0]
    tok = lambda i: (i, 0)
    const = lambda i: (0, 0)
    out_shape = [jax.ShapeDtypeStruct((n, A_WIDTH), BF)] * 3 + [jax.ShapeDtypeStruct((n, A_WIDTH), F32)]
    out_specs = [pl.BlockSpec((tm, A_WIDTH), tok)] * 4
    if emit_cache:
        out_shape += [jax.ShapeDtypeStruct((n, A_WIDTH), F32)] * 2
        out_specs += [pl.BlockSpec((tm, A_WIDTH), tok)] * 2
    return pl.pallas_call(
        functools.partial(_even_in_kernel, layer=layer, j=j),
        grid=(n // tm,),
        in_specs=[
            pl.BlockSpec((tm, D), tok),
            pl.BlockSpec((None, None, 6, D), lambda i: (layer, mod_row(i), 0, 0)),
            pl.BlockSpec(norm_mix.shape, const),
            _resident((None, D, 4 * A_WIDTH), lambda i: (j, 0, 0)),
            _resident((A_WIDTH, A_WIDTH), const),
            pl.BlockSpec(qg.shape, const),
            pl.BlockSpec(kg.shape, const),
        ],
        out_specs=out_specs,
        out_shape=out_shape,
        compiler_params=_cparams(1),
        name="even_in",
    )(x, mod4, norm_mix, w_in, bd, qg, kg)


def _dot_nt(a, b):
    return lax.dot_general(a, b, (((1,), (1,)), ((), ())), preferred_element_type=F32)


def _ctx_attn_kernel(q_ref, k_ref, v_ref, o_ref):
    tq = q_ref.shape[0]
    lo = lax.broadcasted_iota(jnp.int32, (1, LANES), 1) < HEAD_DIM
    for pr in range(N_HEADS // 2):
        sl = slice(pr * LANES, (pr + 1) * LANES)
        q2 = q_ref[:, sl]
        k2 = k_ref[:, sl]
        v2 = v_ref[:, sl]
        zero = jnp.zeros_like(q2)
        qm = jnp.concatenate([jnp.where(lo, q2, zero), jnp.where(lo, zero, q2)], axis=0)
        s = _dot_nt(qm, k2)
        m = jnp.max(s, axis=-1, keepdims=True)
        pe = jnp.exp(s - m)
        den = jnp.sum(pe, axis=-1, keepdims=True)
        o = jnp.dot(pe.astype(BF), v2, preferred_element_type=F32) / den
        o_ref[:, sl] = jnp.where(lo, o[0:tq], o[tq:2 * tq]).astype(o_ref.dtype)


def _ctx_attention(q, k, v, seq):
    n = q.shape[0]
    spec = pl.BlockSpec((seq, A_WIDTH), lambda b: (b, 0))
    return pl.pallas_call(
        _ctx_attn_kernel,
        grid=(n // seq,),
        in_specs=[spec, spec, spec],
        out_specs=spec,
        out_shape=jax.ShapeDtypeStruct((n, A_WIDTH), BF),
        compiler_params=_cparams(1),
        name="ctx_attention",
    )(q, k, v)


def _lat_attn_kernel(q_ref, k_ref, v_ref, ck_ref, cv_ref, b_ref, o_ref, *, rows):
    ws = _group_window_start(pl.program_id(1) * Q_ROWS, rows)
    k0 = pl.multiple_of(ws * GRID_W, GRID_W)
    n_loc = WIN_ROWS * GRID_W
    tq = q_ref.shape[0]
    lo = lax.broadcasted_iota(jnp.int32, (1, LANES), 1) < HEAD_DIM
    for pr in range(N_HEADS // 2):
        sl = slice(pr * LANES, (pr + 1) * LANES)
        q2 = q_ref[:, sl]
        k2 = k_ref[pl.ds(k0, n_loc), sl]
        v2 = v_ref[pl.ds(k0, n_loc), sl]
        ck2 = ck_ref[:, sl].astype(BF)
        cv2 = cv_ref[:, sl].astype(BF)
        zero = jnp.zeros_like(q2)
        qm = jnp.concatenate([jnp.where(lo, q2, zero), jnp.where(lo, zero, q2)], axis=0)
        s_loc = _dot_nt(qm, k2) + b_ref[2 * pr:2 * pr + 2].reshape(2 * tq, n_loc)
        s_ctx = _dot_nt(qm, ck2)
        m = jnp.maximum(jnp.max(s_loc, axis=-1, keepdims=True),
                        jnp.max(s_ctx, axis=-1, keepdims=True))
        p_loc = jnp.exp(s_loc - m)
        p_ctx = jnp.exp(s_ctx - m)
        den = jnp.sum(p_loc, axis=-1, keepdims=True) + jnp.sum(p_ctx, axis=-1, keepdims=True)
        o = (jnp.dot(p_loc.astype(BF), v2, preferred_element_type=F32)
             + jnp.dot(p_ctx.astype(BF), cv2, preferred_element_type=F32)) / den
        o_ref[:, sl] = jnp.where(lo, o[0:tq], o[tq:2 * tq]).astype(o_ref.dtype)


def _lat_attention(q, k, v, cache_k, cache_v, bias, *, j, batch, seq):
    rows = seq // GRID_W
    groups = rows // Q_ROWS
    assert rows % Q_ROWS == 0 and groups >= 3 and rows >= WIN_ROWS
    past = cache_k.shape[2]
    tq = Q_ROWS * GRID_W

    def bias_class(g):
        return jnp.where(g == 0, 0, jnp.where(g == groups - 1, 2, 1))

    kv_spec = pl.BlockSpec((seq, A_WIDTH), lambda b, g: (b, 0))
    cache_spec = pl.BlockSpec((None, None, past, A_WIDTH), lambda b, g: (b, j, 0, 0))
    return pl.pallas_call(
        functools.partial(_lat_attn_kernel, rows=rows),
        grid=(batch, groups),
        in_specs=[
            pl.BlockSpec((tq, A_WIDTH), lambda b, g: (b * groups + g, 0)),
            kv_spec, kv_spec, cache_spec, cache_spec,
            pl.BlockSpec((None, None, N_HEADS, tq, WIN_ROWS * GRID_W),
                         lambda b, g: (j, bias_class(g), 0, 0, 0)),
        ],
        out_specs=pl.BlockSpec((tq, A_WIDTH), lambda b, g: (b * groups + g, 0)),
        out_shape=jax.ShapeDtypeStruct((batch * seq, A_WIDTH), BF),
        compiler_params=_cparams(2),
        name="lat_attention",
    )(q, k, v, cache_k, cache_v, bias)


def _even_out_kernel(x_ref, a_ref, p_ref, pp_ref, pn_ref, mod_ref, pw_ref, ps_ref, wo_ref,
                     o_ref, ext_ref, *, j, tm, tiles_per_seq, seq):
    tis = pl.program_id(0) % tiles_per_seq
    halo = pp_ref.shape[0]
    ext_ref[0:halo, :] = jnp.where(tis == 0, 0.0, pp_ref[...])
    ext_ref[halo:halo + tm, :] = p_ref[...]
    ext_ref[halo + tm:2 * halo + tm, :] = jnp.where(tis == tiles_per_seq - 1, 0.0, pn_ref[...])
    t = tis * tm + lax.broadcasted_iota(jnp.int32, (tm, 1), 0)
    pooled = []
    for g, w in enumerate(POOL_WINDOWS):
        sl = slice(g * POOL_GROUP, (g + 1) * POOL_GROUP)
        acc = None
        for off in range(-(w // 2), w // 2):
            term = ext_ref[halo + off:halo + off + tm, sl]
            acc = term if acc is None else acc + term
        cnt = (jnp.minimum(t + (w // 2 - 1), seq - 1) - jnp.maximum(t - w // 2, 0) + 1).astype(F32)
        d = acc / cnt - p_ref[:, sl]
        y = jnp.dot(d.astype(BF), pw_ref[g], preferred_element_type=F32)
        pooled.append(y * ps_ref[j:j + 1, sl])
    pooled = jnp.concatenate(pooled, axis=-1).astype(BF)
    y = (jnp.dot(a_ref[...], wo_ref[0:A_WIDTH, :], preferred_element_type=F32)
         + jnp.dot(pooled, wo_ref[A_WIDTH:, :], preferred_element_type=F32))
    o_ref[...] = x_ref[...] + mod_ref[2:3, :] * y


def _halo_specs(width, tm, n, halo):
    blocks = tm // halo
    last = n // halo - 1
    prev = pl.BlockSpec((halo, width), lambda i: (jnp.maximum(i * blocks - 1, 0), 0))
    nxt = pl.BlockSpec((halo, width), lambda i: (jnp.minimum((i + 1) * blocks, last), 0))
    return prev, nxt


def _even_out(x, attn, p, mod4, mod_row, pool_w, pool_scale, w_out, *, layer, j, tm, seq):
    n = x.shape[0]
    tok = lambda i: (i, 0)
    const = lambda i: (0, 0)
    halo = 8
    prev, nxt = _halo_specs(A_WIDTH, tm, n, halo)
    return pl.pallas_call(
        functools.partial(_even_out_kernel, j=j, tm=tm, tiles_per_seq=seq // tm, seq=seq),
        grid=(n // tm,),
        in_specs=[
            pl.BlockSpec((tm, D), tok),
            pl.BlockSpec((tm, A_WIDTH), tok),
            pl.BlockSpec((tm, A_WIDTH), tok),
            prev, nxt,
            pl.BlockSpec((None, None, 6, D), lambda i: (layer, mod_row(i), 0, 0)),
            _resident((None, len(POOL_WINDOWS), POOL_GROUP, POOL_GROUP), lambda i: (j, 0, 0, 0)),
            pl.BlockSpec(pool_scale.shape, const),
            _resident((None, D, D), lambda i: (j, 0, 0)),
        ],
        out_specs=pl.BlockSpec((tm, D), tok),
        out_shape=jax.ShapeDtypeStruct((n, D), F32),
        scratch_shapes=[pltpu.VMEM((tm + 2 * halo, A_WIDTH), F32)],
        compiler_params=_cparams(1),
        name="even_out",
    )(x, attn, p, p, p, mod4, pool_w, pool_scale, w_out)


def _conf_glu_kernel(x_ref, xp_ref, xn_ref, mod_ref, nmix_ref, w1_ref, gb_ref, h_ref, g_ref,
                     *, layer, tm, nsub, tiles_per_seq):
    g = nmix_ref[layer:layer + 1, :]
    sh = mod_ref[0:1, :]
    sc = mod_ref[1:2, :]
    halo = xp_ref.shape[0]
    sub_len = tm // nsub
    hm = _modnorm(x_ref[...], g, sh, sc)
    if nsub == 1:
        tis = pl.program_id(0) % tiles_per_seq
        hp = jnp.where(tis == 0, 0.0, _modnorm(xp_ref[...], g, sh, sc))
        hn = jnp.where(tis == tiles_per_seq - 1, 0.0, _modnorm(xn_ref[...], g, sh, sc))
        pieces = [hp, hm, hn]
    else:
        z = jnp.zeros((halo, D), F32)
        pieces = [z]
        for q in range(nsub):
            pieces += [hm[q * sub_len:(q + 1) * sub_len], z]
    h_ref[...] = jnp.concatenate(pieces, axis=0).astype(BF)

    seg = tm // CONV_SEGS
    seg_ext = seg + 2 * halo
    gp = _seg_pitch(seg_ext)
    segs_per_sub = CONV_SEGS // nsub
    halves = FF_CHUNK // LANES
    half_segs = CONV_SEGS // 2
    for c in range(D // FF_CHUNK):
        a = jnp.dot(h_ref[...], w1_ref[:, c * FF_CHUNK:(c + 1) * FF_CHUNK],
                    preferred_element_type=F32)
        gate = jnp.dot(h_ref[...], w1_ref[:, D + c * FF_CHUNK:D + (c + 1) * FF_CHUNK],
                       preferred_element_type=F32)
        glu = a * _sigmoid(gate)
        for hf in range(halves):
            slab = c * halves + hf
            for s in range(CONV_SEGS):
                start = s * seg + halo * (s // segs_per_sub)
                g_ref[slab, s * gp:s * gp + seg_ext, :] = (
                    glu[start:start + seg_ext, hf * LANES:(hf + 1) * LANES])
            for a0 in range(seg_ext):
                lo = g_ref[slab, pl.ds(a0, SUBLANES, stride=gp), :]
                hi = g_ref[slab, pl.ds(half_segs * gp + a0, SUBLANES, stride=gp), :]
                gb_ref[slab, a0 * CONV_SEGS:(a0 + 1) * CONV_SEGS, :] = (
                    jnp.concatenate([lo, hi], axis=0).astype(BF))


def _conf_conv_kernel(gb_ref, dw_ref, dwb_ref, u_ref, wb_ref, *, j, seg, off):
    @pl.when(pl.program_id(0) == 0)
    def _():
        for k in range(CONV_K):
            wk = jnp.broadcast_to(dw_ref[k:k + 1, :], (CONV_SEGS, D)).astype(BF)
            for slab in range(D // LANES):
                wb_ref[slab, k * CONV_SEGS:(k + 1) * CONV_SEGS, :] = wk[:, slab * LANES:(slab + 1) * LANES]

    for slab in range(D // LANES):
        bias = dwb_ref[j:j + 1, slab * LANES:(slab + 1) * LANES]
        for a0 in range(seg):
            acc = jnp.zeros((CONV_SEGS, LANES), F32)
            for k in range(CONV_K):
                t0 = (a0 + k + off) * CONV_SEGS
                acc = acc + (gb_ref[slab, t0:t0 + CONV_SEGS, :].astype(F32)
                             * wb_ref[slab, k * CONV_SEGS:(k + 1) * CONV_SEGS, :].astype(F32))
            u_ref[slab, a0 * CONV_SEGS:(a0 + 1) * CONV_SEGS, :] = (acc + bias).astype(u_ref.dtype)


def _conf_out_kernel(u_ref, x_ref, mod_ref, lng_ref, lnb_ref, w2_ref, o_ref, y_ref, *, j, seg):
    n_slabs = D // LANES
    u = jnp.concatenate([u_ref[slab] for slab in range(n_slabs)], axis=-1).astype(F32)
    mu = jnp.mean(u, axis=-1, keepdims=True)
    uc = u - mu
    var = jnp.mean(uc * uc, axis=-1, keepdims=True)
    y = uc * lax.rsqrt(var + EPS) * lng_ref[j:j + 1, :] + lnb_ref[j:j + 1, :]
    y = y * _sigmoid(y)
    y = jnp.dot(y.astype(BF), w2_ref[...], preferred_element_type=F32)
    for slab in range(n_slabs):
        y_ref[slab] = y[:, slab * LANES:(slab + 1) * LANES]
    gate_mix = mod_ref[2:3, :]
    for s in range(CONV_SEGS):
        ys = jnp.concatenate(
            [y_ref[slab, pl.ds(s, seg, stride=CONV_SEGS), :] for slab in range(n_slabs)], axis=-1)
        rows = slice(s * seg, (s + 1) * seg)
        o_ref[rows, :] = x_ref[rows, :] + gate_mix * ys


def _seg_pitch(n):
    p = -(-n // 4)
    return 4 * (p if p % 2 else p + 1)


def _conformer(x, mod4, mod_row, norm_mix, w1, dw, dwb, lng, lnb, w2, *, layer, j, tm, seq):
    n = x.shape[0]
    tok = lambda i: (i, 0)
    const = lambda i: (0, 0)
    halo = 16
    prev, nxt = _halo_specs(D, tm, n, halo)
    nsub = max(tm // seq, 1)
    assert tm % seq == 0 or seq % tm == 0
    assert CONV_SEGS == 2 * SUBLANES and CONV_SEGS % nsub == 0 and halo >= CONV_K // 2
    seg = tm // CONV_SEGS
    seg_ext = seg + 2 * halo
    gp = _seg_pitch(seg_ext)
    ext_rows = tm + halo * (nsub + 1)
    n_tiles = n // tm
    n_slabs = D // LANES
    mod_spec = pl.BlockSpec((None, None, 6, D), lambda i: (layer, mod_row(i), 0, 0))
    gb_spec = pl.BlockSpec((None, n_slabs, seg_ext * CONV_SEGS, LANES), lambda i: (i, 0, 0, 0))
    u_spec = pl.BlockSpec((None, n_slabs, tm, LANES), lambda i: (i, 0, 0, 0))

    gb = pl.pallas_call(
        functools.partial(_conf_glu_kernel, layer=layer, tm=tm, nsub=nsub,
                          tiles_per_seq=max(seq // tm, 1)),
        grid=(n_tiles,),
        in_specs=[
            pl.BlockSpec((tm, D), tok), prev, nxt, mod_spec,
            pl.BlockSpec(norm_mix.shape, const),
            _resident((None, D, 2 * D), lambda i: (j, 0, 0)),
        ],
        out_specs=gb_spec,
        out_shape=jax.ShapeDtypeStruct((n_tiles, n_slabs, seg_ext * CONV_SEGS, LANES), BF),
        scratch_shapes=[
            pltpu.VMEM((ext_rows, D), BF),
            pltpu.VMEM((n_slabs, CONV_SEGS * gp, LANES), F32),
        ],
        compiler_params=_cparams(1),
        name="conf_glu",
    )(x, x, x, mod4, norm_mix, w1)

    u = pl.pallas_call(
        functools.partial(_conf_conv_kernel, j=j, seg=seg, off=halo - CONV_K // 2),
        grid=(n_tiles,),
        in_specs=[
            gb_spec,
            pl.BlockSpec((None, CONV_K, D), lambda i: (j, 0, 0)),
            pl.BlockSpec(dwb.shape, const),
        ],
        out_specs=u_spec,
        out_shape=jax.ShapeDtypeStruct((n_tiles, n_slabs, tm, LANES), BF),
        scratch_shapes=[pltpu.VMEM((n_slabs, CONV_K * CONV_SEGS, LANES), BF)],
        compiler_params=_cparams(1),
        name="conf_conv",
    )(gb, dw, dwb)

    return pl.pallas_call(
        functools.partial(_conf_out_kernel, j=j, seg=seg),
        grid=(n_tiles,),
        in_specs=[
            u_spec,
            pl.BlockSpec((tm, D), tok), mod_spec,
            pl.BlockSpec(lng.shape, const),
            pl.BlockSpec(lnb.shape, const),
            _resident((None, D, D), lambda i: (j, 0, 0)),
        ],
        out_specs=pl.BlockSpec((tm, D), tok),
        out_shape=jax.ShapeDtypeStruct((n, D), F32),
        scratch_shapes=[pltpu.VMEM((n_slabs, tm, LANES), F32)],
        compiler_params=_cparams(1),
        name="conf_out",
    )(u, x, mod4, lng, lnb, w2)


def _ffn_kernel(x_ref, xp_ref, xn_ref, mod_ref, nffn_ref, wi_ref, cw_ref, cb_ref, wo_ref,
                o_ref, h_ref, act_ref, *, tm, nsub, tiles_per_seq, layer):
    g = nffn_ref[layer:layer + 1, :]
    sh = mod_ref[3:4, :]
    sc = mod_ref[4:5, :]
    halo = xp_ref.shape[0]
    sub_len = tm // nsub
    hm = _modnorm(x_ref[...], g, sh, sc)
    if nsub == 1:
        tis = pl.program_id(0) % tiles_per_seq
        hp = jnp.where(tis == 0, 0.0, _modnorm(xp_ref[...], g, sh, sc))
        hn = jnp.where(tis == tiles_per_seq - 1, 0.0, _modnorm(xn_ref[...], g, sh, sc))
        z = jnp.zeros((FFN_PAD - halo, D), F32)
        pieces = [z, hp, hm, hn, z]
    else:
        z = jnp.zeros((FFN_PAD, D), F32)
        pieces = [z]
        for q in range(nsub):
            pieces += [hm[q * sub_len:(q + 1) * sub_len], z]
    h_ref[...] = jnp.concatenate(pieces, axis=0).astype(BF)
    n_ext = tm + FFN_PAD * (nsub + 1)

    def conv(col):
        u = jnp.dot(h_ref[...], wi_ref[:, col:col + FF_CHUNK], preferred_element_type=F32)
        um = pltpu.roll(u, 1, 0)
        up = pltpu.roll(u, n_ext - 1, 0)
        return (cw_ref[0:1, col:col + FF_CHUNK] * um
                + cw_ref[1:2, col:col + FF_CHUNK] * u
                + cw_ref[2:3, col:col + FF_CHUNK] * up
                + cb_ref[layer:layer + 1, col:col + FF_CHUNK])

    for c in range(D_FF // FF_CHUNK):
        a = conv(c * FF_CHUNK)
        gate = conv(D_FF + c * FF_CHUNK)
        act = (a * _sigmoid(a) * gate).astype(BF)
        for q in range(nsub):
            r0 = FFN_PAD + q * (sub_len + FFN_PAD)
            act_ref[q * sub_len:(q + 1) * sub_len, c * FF_CHUNK:(c + 1) * FF_CHUNK] = act[r0:r0 + sub_len]
    y = jnp.dot(act_ref[...], wo_ref[...], preferred_element_type=F32)
    o_ref[...] = x_ref[...] + mod_ref[5:6, :] * y


def _ffn(x, mod4, mod_row, norm_ffn, w_in, conv_w, conv_b, w_out, *, layer, tm, seq):
    n = x.shape[0]
    tok = lambda i: (i, 0)
    const = lambda i: (0, 0)
    halo = 8
    prev, nxt = _halo_specs(D, tm, n, halo)
    nsub = max(tm // seq, 1)
    assert tm % seq == 0 or seq % tm == 0
    return pl.pallas_call(
        functools.partial(_ffn_kernel, layer=layer, tm=tm, nsub=nsub,
                          tiles_per_seq=max(seq // tm, 1)),
        grid=(n // tm,),
        in_specs=[
            pl.BlockSpec((tm, D), tok), prev, nxt,
            pl.BlockSpec((None, None, 6, D), lambda i: (layer, mod_row(i), 0, 0)),
            pl.BlockSpec(norm_ffn.shape, const),
            _resident((None, D, 2 * D_FF), lambda i: (layer, 0, 0)),
            pl.BlockSpec((None, 3, 2 * D_FF), lambda i: (layer, 0, 0)),
            pl.BlockSpec(conv_b.shape, const),
            _resident((None, D_FF, D), lambda i: (layer, 0, 0)),
        ],
        out_specs=pl.BlockSpec((tm, D), tok),
        out_shape=jax.ShapeDtypeStruct((n, D), F32),
        scratch_shapes=[
            pltpu.VMEM((tm + FFN_PAD * (nsub + 1), D), BF),
            pltpu.VMEM((tm, D_FF), BF),
        ],
        compiler_params=_cparams(1),
        name="conv_ffn",
    )(x, x, x, mod4, norm_ffn, w_in, conv_w, conv_b, w_out)


def kernel(x_prompt, x_sample, cache_k, cache_v, c, c_ctx, norm_mix, norm_ffn, w_mod, b_mod,
           w_in_ab, q_gain, k_gain, rpb, pool_w, pool_scale, w_out_ab,
           conv_pw1, conv_dw, conv_dw_b, conv_ln_g, conv_ln_b, conv_pw2,
           ffn_w_in, ffn_conv_w, ffn_conv_b, ffn_w_out):
    batch, seq, _ = x_prompt.shape
    dec_batch, dec_seq, _ = x_sample.shape
    n_even = w_in_ab.shape[0]
    past = cache_k.shape[2]

    xc = x_prompt.reshape(batch * seq, D)
    xl = x_sample.reshape(dec_batch * dec_seq, D)

    ctx_row = dec_batch
    c_all = jnp.concatenate([c, c_ctx[None, :], jnp.zeros((16 - dec_batch - 1, D), F32)], axis=0)
    mod4 = _modulation(c_all, w_mod, b_mod).reshape(DEPTH, 16, 6, D)
    bias = _bias_table(rpb, dec_seq // GRID_W)

    tm_c = seq
    tm_l = 1024
    tm_conf = 512
    tm_ffn = 1024
    row_c = lambda i: ctx_row
    row_l = lambda tm: (lambda i: i // (dec_seq // tm))

    head_ids = jnp.arange(A_WIDTH) // HEAD_DIM
    bd = (head_ids[:, None] == head_ids[None, :]).astype(BF)
    qg = jnp.tile(q_gain, (1, N_HEADS))
    kg = jnp.tile(k_gain, (1, N_HEADS))
    ck = cache_k.reshape(dec_batch, n_even, past, A_WIDTH)
    cv = cache_v.reshape(dec_batch, n_even, past, A_WIDTH)

    w_in_ab_b = w_in_ab.astype(BF)
    w_out_ab_b = w_out_ab.astype(BF)
    pool_w_b = pool_w.astype(BF)
    pw1_b = conv_pw1.astype(BF)
    pw2_b = conv_pw2.astype(BF)
    ffn_wi_b = ffn_w_in.astype(BF)
    ffn_wo_b = ffn_w_out.astype(BF)

    new_k, new_v = [], []
    for layer in range(DEPTH):
        j = layer // 2
        if layer % 2 == 0:
            qc, kc, vc, pc, kf, vf = _even_in(xc, mod4, row_c, norm_mix, w_in_ab_b, bd, qg, kg,
                                              layer=layer, j=j, tm=tm_c, emit_cache=True)
            ql, kl, vl, p_l = _even_in(xl, mod4, row_l(tm_l), norm_mix, w_in_ab_b, bd, qg, kg,
                                       layer=layer, j=j, tm=tm_l, emit_cache=False)
            ac = _ctx_attention(qc, kc, vc, seq)
            al = _lat_attention(ql, kl, vl, ck, cv, bias, j=j, batch=dec_batch, seq=dec_seq)
            xc = _even_out(xc, ac, pc, mod4, row_c, pool_w_b, pool_scale, w_out_ab_b,
                           layer=layer, j=j, tm=tm_c, seq=seq)
            xl = _even_out(xl, al, p_l, mod4, row_l(tm_l), pool_w_b, pool_scale, w_out_ab_b,
                           layer=layer, j=j, tm=tm_l, seq=dec_seq)
            new_k.append(kf.reshape(batch, seq, N_HEADS, HEAD_DIM))
            new_v.append(vf.reshape(batch, seq, N_HEADS, HEAD_DIM))
        else:
            xc = _conformer(xc, mod4, row_c, norm_mix, pw1_b, conv_dw, conv_dw_b, conv_ln_g,
                            conv_ln_b, pw2_b, layer=layer, j=j, tm=tm_conf, seq=seq)
            xl = _conformer(xl, mod4, row_l(tm_conf), norm_mix, pw1_b, conv_dw, conv_dw_b,
                            conv_ln_g, conv_ln_b, pw2_b, layer=layer, j=j, tm=tm_conf, seq=dec_seq)
        xc = _ffn(xc, mod4, row_c, norm_ffn, ffn_wi_b, ffn_conv_w, ffn_conv_b, ffn_wo_b,
                  layer=layer, tm=tm_ffn, seq=seq)
        xl = _ffn(xl, mod4, row_l(tm_ffn), norm_ffn, ffn_wi_b, ffn_conv_w, ffn_conv_b, ffn_wo_b,
                  layer=layer, tm=tm_ffn, seq=dec_seq)

    return (xc.reshape(batch, seq, D), xl.reshape(dec_batch, dec_seq, D),
            jnp.stack(new_k, axis=1), jnp.stack(new_v, axis=1))
```

```python
import functools

import jax
import jax.numpy as jnp
from jax import lax
from jax.experimental import pallas as pl
from jax.experimental.pallas import tpu as pltpu

D = 1024
DEPTH = 4
GRID_W = 64
A_WIDTH = 512
HEAD_DIM = 64
N_HEADS = 8
WIN_H = 8
WIN_W = 16
POOL_WINDOWS = (2, 4, 8, 16)
POOL_GROUP = 128
CONV_K = 31
D_FF = 2816
EPS = 1e-6
NEG_INF = -1e30

F32 = jnp.float32
BF = jnp.bfloat16

LANES = 128
SUBLANES = 8
CONV_SEGS = 16
FFN_PAD = 16
FF_CHUNK = 256
VMEM_LIMIT = 56 * 1024 * 1024


def _cparams(n_grid):
    return pltpu.CompilerParams(
        dimension_semantics=("arbitrary",) * n_grid, vmem_limit_bytes=VMEM_LIMIT)


def _resident(shape, index_map):
    return pl.BlockSpec(shape, index_map, pipeline_mode=pl.Buffered(1))


def _sigmoid(x):
    return 1.0 / (1.0 + jnp.exp(-x))


def _modnorm(x, g, shift, scale):
    ms = jnp.mean(x * x, axis=-1, keepdims=True)
    return (x * lax.rsqrt(ms + EPS) * g) * (1.0 + scale) + shift


def _mod_kernel(c_ref, w_ref, b_ref, o_ref):
    c = c_ref[...]
    s = c * _sigmoid(c)
    o_ref[...] = jnp.dot(s.astype(BF), w_ref[...].astype(BF),
                         preferred_element_type=F32) + b_ref[...]


def _modulation(c_all, w_mod, b_mod):
    tn = 1536
    return pl.pallas_call(
        _mod_kernel,
        grid=(DEPTH, 6 * D // tn),
        in_specs=[
            pl.BlockSpec((16, D), lambda l, n: (0, 0)),
            pl.BlockSpec((None, D, tn), lambda l, n: (l, 0, n)),
            pl.BlockSpec((None, 1, tn), lambda l, n: (l, 0, n)),
        ],
        out_specs=pl.BlockSpec((None, 16, tn), lambda l, n: (l, 0, n)),
        out_shape=jax.ShapeDtypeStruct((DEPTH, 16, 6 * D), F32),
        compiler_params=_cparams(2),
        name="modulation",
    )(c_all, w_mod, b_mod.reshape(DEPTH, 1, 6 * D))


Q_ROWS = WIN_H // 2
WIN_ROWS = Q_ROWS + WIN_H - 1
N_BIAS_CLASSES = 3


def _group_window_start(r0, rows):
    return jnp.clip(r0 - WIN_H // 2, 0, rows - WIN_ROWS)


def _bias_kernel(b_ref, o_ref, t_ref, *, rows):
    cls = pl.program_id(1)
    q = lax.broadcasted_iota(jnp.int32, (GRID_W, LANES), 0)
    l = lax.broadcasted_iota(jnp.int32, (GRID_W, LANES), 1)
    n_rho = 2 * WIN_H - 1
    lo = l < GRID_W

    @pl.when(cls == 0)
    def _():
        ck = l & (GRID_W - 1)
        cs = jnp.clip(q - WIN_W // 2, 0, GRID_W - WIN_W)
        valid = (ck >= cs) & (ck < cs + WIN_W)

        def body(hr, carry):
            row = jnp.broadcast_to(b_ref[pl.ds(hr, 1), :], (GRID_W, LANES))
            r_lo = pltpu.roll(row, 0, 1, stride=1, stride_axis=0)
            r_hi = pltpu.roll(row, GRID_W, 1, stride=1, stride_axis=0)
            t_ref[hr] = jnp.where(valid, jnp.where(lo, r_lo, r_hi), NEG_INF)
            return carry

        lax.fori_loop(0, N_HEADS * n_rho, body, 0, unroll=8)

    r0 = jnp.where(cls == 0, 0, jnp.where(cls == 1, Q_ROWS, rows - Q_ROWS))
    ws = _group_window_start(r0, rows)
    neg = jnp.full((GRID_W, LANES), NEG_INF, F32)
    for h in range(N_HEADS):
        for rq in range(Q_ROWS):
            r = r0 + rq
            rs = jnp.clip(r - WIN_H // 2, 0, rows - WIN_H)

            def blk(kappa):
                ka = ws + kappa
                ok = (ka >= rs) & (ka < rs + WIN_H)
                rho = jnp.clip(ka - r + (WIN_H - 1), 0, n_rho - 1)
                return jnp.where(ok, t_ref[h * n_rho + rho], neg)

            qs = slice(rq * GRID_W, (rq + 1) * GRID_W)
            for pr in range(WIN_ROWS // 2):
                o_ref[h, qs, pr * LANES:(pr + 1) * LANES] = jnp.where(lo, blk(2 * pr), blk(2 * pr + 1))
            if WIN_ROWS % 2:
                tail = (WIN_ROWS - 1) * GRID_W
                o_ref[h, qs, tail:tail + GRID_W] = blk(WIN_ROWS - 1)[:, 0:GRID_W]


def _bias_table(rpb, rows):
    n_even = rpb.shape[0]
    blk = (N_HEADS, Q_ROWS * GRID_W, WIN_ROWS * GRID_W)
    n_hr = N_HEADS * (2 * WIN_H - 1)
    n_d = 2 * WIN_W - 1
    rp = rpb.reshape(n_even, n_hr, n_d)
    b_rows = jnp.concatenate(
        [rp[..., WIN_W - 1:], jnp.zeros((n_even, n_hr, LANES - n_d), F32), rp[..., :WIN_W - 1]], axis=-1)
    return pl.pallas_call(
        functools.partial(_bias_kernel, rows=rows),
        grid=(n_even, N_BIAS_CLASSES),
        in_specs=[pl.BlockSpec((None, n_hr, LANES), lambda j, c: (j, 0, 0))],
        out_specs=pl.BlockSpec((None, None) + blk, lambda j, c: (j, c, 0, 0, 0)),
        out_shape=jax.ShapeDtypeStruct((n_even, N_BIAS_CLASSES) + blk, F32),
        scratch_shapes=[pltpu.VMEM((N_HEADS * (2 * WIN_H - 1), GRID_W, LANES), F32)],
        compiler_params=_cparams(2),
        name="bias_table",
    )(b_rows)


def _even_in_kernel(x_ref, mod_ref, nmix_ref, w_ref, bd_ref, qg_ref, kg_ref,
                    q_ref, k_ref, v_ref, p_ref, *cache_refs, layer, j):
    g = nmix_ref[layer:layer + 1, :]
    h = _modnorm(x_ref[...], g, mod_ref[0:1, :], mod_ref[1:2, :]).astype(BF)
    proj = jnp.dot(h, w_ref[...], preferred_element_type=F32)

    def headnorm(t, gain):
        ss = jnp.dot((t * t).astype(BF), bd_ref[...], preferred_element_type=F32)
        return t * lax.rsqrt(ss * (1.0 / HEAD_DIM) + EPS) * gain

    q = headnorm(proj[:, 0:A_WIDTH], qg_ref[j:j + 1, :])
    k = headnorm(proj[:, A_WIDTH:2 * A_WIDTH], kg_ref[j:j + 1, :])
    v = proj[:, 2 * A_WIDTH:3 * A_WIDTH]
    q_ref[...] = (q * (HEAD_DIM ** -0.5)).astype(BF)
    k_ref[...] = k.astype(BF)
    v_ref[...] = v.astype(BF)
    p_ref[...] = proj[:, 3 * A_WIDTH:]
    if cache_refs:
        cache_refs[0][...] = k
        cache_refs[1][...] = v


def _even_in(x, mod4, mod_row, norm_mix, w_in, bd, qg, kg, *, layer, j, tm, emit_cache):
    n = x.shape[0]
    tok = lambda i: (i, 0)
    const = lambda i: (0, 0)
    out_shape = [jax.ShapeDtypeStruct((n, A_WIDTH), BF)] * 3 + [jax.ShapeDtypeStruct((n, A_WIDTH), F32)]
    out_specs = [pl.BlockSpec((tm, A_WIDTH), tok)] * 4
    if emit_cache:
        out_shape += [jax.ShapeDtypeStruct((n, A_WIDTH), F32)] * 2
        out_specs += [pl.BlockSpec((tm, A_WIDTH), tok)] * 2
    return pl.pallas_call(
        functools.partial(_even_in_kernel, layer=layer, j=j),
        grid=(n // tm,),
        in_specs=[
            pl.BlockSpec((tm, D), tok),
            pl.BlockSpec((None, None, 6, D), lambda i: (layer, mod_row(i), 0, 0)),
            pl.BlockSpec(norm_mix.shape, const),
            _resident((None, D, 4 * A_WIDTH), lambda i: (j, 0, 0)),
            _resident((A_WIDTH, A_WIDTH), const),
            pl.BlockSpec(qg.shape, const),
            pl.BlockSpec(kg.shape, const),
        ],
        out_specs=out_specs,
        out_shape=out_shape,
        compiler_params=_cparams(1),
        name="even_in",
    )(x, mod4, norm_mix, w_in, bd, qg, kg)


def _dot_nt(a, b):
    return lax.dot_general(a, b, (((1,), (1,)), ((), ())), preferred_element_type=F32)


def _ctx_attn_kernel(q_ref, k_ref, v_ref, o_ref):
    tq = q_ref.shape[0]
    lo = lax.broadcasted_iota(jnp.int32, (1, LANES), 1) < HEAD_DIM
    for pr in range(N_HEADS // 2):
        sl = slice(pr * LANES, (pr + 1) * LANES)
        q2 = q_ref[:, sl]
        k2 = k_ref[:, sl]
        v2 = v_ref[:, sl]
        zero = jnp.zeros_like(q2)
        qm = jnp.concatenate([jnp.where(lo, q2, zero), jnp.where(lo, zero, q2)], axis=0)
        s = _dot_nt(qm, k2)
        m = jnp.max(s, axis=-1, keepdims=True)
        pe = jnp.exp(s - m)
        den = jnp.sum(pe, axis=-1, keepdims=True)
        o = jnp.dot(pe.astype(BF), v2, preferred_element_type=F32) / den
        o_ref[:, sl] = jnp.where(lo, o[0:tq], o[tq:2 * tq]).astype(o_ref.dtype)


def _ctx_attention(q, k, v, seq):
    n = q.shape[0]
    spec = pl.BlockSpec((seq, A_WIDTH), lambda b: (b, 0))
    return pl.pallas_call(
        _ctx_attn_kernel,
        grid=(n // seq,),
        in_specs=[spec, spec, spec],
        out_specs=spec,
        out_shape=jax.ShapeDtypeStruct((n, A_WIDTH), BF),
        compiler_params=_cparams(1),
        name="ctx_attention",
    )(q, k, v)


def _lat_attn_kernel(q_ref, k_ref, v_ref, ck_ref, cv_ref, b_ref, o_ref, *, rows):
    ws = _group_window_start(pl.program_id(1) * Q_ROWS, rows)
    k0 = pl.multiple_of(ws * GRID_W, GRID_W)
    n_loc = WIN_ROWS * GRID_W
    tq = q_ref.shape[0]
    lo = lax.broadcasted_iota(jnp.int32, (1, LANES), 1) < HEAD_DIM
    for pr in range(N_HEADS // 2):
        sl = slice(pr * LANES, (pr + 1) * LANES)
        q2 = q_ref[:, sl]
        k2 = k_ref[pl.ds(k0, n_loc), sl]
        v2 = v_ref[pl.ds(k0, n_loc), sl]
        ck2 = ck_ref[:, sl].astype(BF)
        cv2 = cv_ref[:, sl].astype(BF)
        zero = jnp.zeros_like(q2)
        qm = jnp.concatenate([jnp.where(lo, q2, zero), jnp.where(lo, zero, q2)], axis=0)
        s_loc = _dot_nt(qm, k2) + b_ref[2 * pr:2 * pr + 2].reshape(2 * tq, n_loc)
        s_ctx = _dot_nt(qm, ck2)
        m = jnp.maximum(jnp.max(s_loc, axis=-1, keepdims=True),
                        jnp.max(s_ctx, axis=-1, keepdims=True))
        p_loc = jnp.exp(s_loc - m)
        p_ctx = jnp.exp(s_ctx - m)
        den = jnp.sum(p_loc, axis=-1, keepdims=True) + jnp.sum(p_ctx, axis=-1, keepdims=True)
        o = (jnp.dot(p_loc.astype(BF), v2, preferred_element_type=F32)
             + jnp.dot(p_ctx.astype(BF), cv2, preferred_element_type=F32)) / den
        o_ref[:, sl] = jnp.where(lo, o[0:tq], o[tq:2 * tq]).astype(o_ref.dtype)


def _lat_attention(q, k, v, cache_k, cache_v, bias, *, j, batch, seq):
    rows = seq // GRID_W
    groups = rows // Q_ROWS
    assert rows % Q_ROWS == 0 and groups >= 3 and rows >= WIN_ROWS
    past = cache_k.shape[2]
    tq = Q_ROWS * GRID_W

    def bias_class(g):
        return jnp.where(g == 0, 0, jnp.where(g == groups - 1, 2, 1))

    kv_spec = pl.BlockSpec((seq, A_WIDTH), lambda b, g: (b, 0))
    cache_spec = pl.BlockSpec((None, None, past, A_WIDTH), lambda b, g: (b, j, 0, 0))
    return pl.pallas_call(
        functools.partial(_lat_attn_kernel, rows=rows),
        grid=(batch, groups),
        in_specs=[
            pl.BlockSpec((tq, A_WIDTH), lambda b, g: (b * groups + g, 0)),
            kv_spec, kv_spec, cache_spec, cache_spec,
            pl.BlockSpec((None, None, N_HEADS, tq, WIN_ROWS * GRID_W),
                         lambda b, g: (j, bias_class(g), 0, 0, 0)),
        ],
        out_specs=pl.BlockSpec((tq, A_WIDTH), lambda b, g: (b * groups + g, 0)),
        out_shape=jax.ShapeDtypeStruct((batch * seq, A_WIDTH), BF),
        compiler_params=_cparams(2),
        name="lat_attention",
    )(q, k, v, cache_k, cache_v, bias)


def _even_out_kernel(x_ref, a_ref, p_ref, pp_ref, pn_ref, mod_ref, pw_ref, ps_ref, wo_ref,
                     o_ref, ext_ref, *, j, tm, tiles_per_seq, seq):
    tis = pl.program_id(0) % tiles_per_seq
    halo = pp_ref.shape[0]
    ext_ref[0:halo, :] = jnp.where(tis == 0, 0.0, pp_ref[...])
    ext_ref[halo:halo + tm, :] = p_ref[...]
    ext_ref[halo + tm:2 * halo + tm, :] = jnp.where(tis == tiles_per_seq - 1, 0.0, pn_ref[...])
    t = tis * tm + lax.broadcasted_iota(jnp.int32, (tm, 1), 0)
    pooled = []
    for g, w in enumerate(POOL_WINDOWS):
        sl = slice(g * POOL_GROUP, (g + 1) * POOL_GROUP)
        acc = None
        for off in range(-(w // 2), w // 2):
            term = ext_ref[halo + off:halo + off + tm, sl]
            acc = term if acc is None else acc + term
        cnt = (jnp.minimum(t + (w // 2 - 1), seq - 1) - jnp.maximum(t - w // 2, 0) + 1).astype(F32)
        d = acc / cnt - p_ref[:, sl]
        y = jnp.dot(d.astype(BF), pw_ref[g], preferred_element_type=F32)
        pooled.append(y * ps_ref[j:j + 1, sl])
    pooled = jnp.concatenate(pooled, axis=-1).astype(BF)
    y = (jnp.dot(a_ref[...], wo_ref[0:A_WIDTH, :], preferred_element_type=F32)
         + jnp.dot(pooled, wo_ref[A_WIDTH:, :], preferred_element_type=F32))
    o_ref[...] = x_ref[...] + mod_ref[2:3, :] * y


def _halo_specs(width, tm, n, halo):
    blocks = tm // halo
    last = n // halo - 1
    prev = pl.BlockSpec((halo, width), lambda i: (jnp.maximum(i * blocks - 1, 0), 0))
    nxt = pl.BlockSpec((halo, width), lambda i: (jnp.minimum((i + 1) * blocks, last), 0))
    return prev, nxt


def _even_out(x, attn, p, mod4, mod_row, pool_w, pool_scale, w_out, *, layer, j, tm, seq):
    n = x.shape[0]
    tok = lambda i: (i, 0)
    const = lambda i: (0, 0)
    halo = 8
    prev, nxt = _halo_specs(A_WIDTH, tm, n, halo)
    return pl.pallas_call(
        functools.partial(_even_out_kernel, j=j, tm=tm, tiles_per_seq=seq // tm, seq=seq),
        grid=(n // tm,),
        in_specs=[
            pl.BlockSpec((tm, D), tok),
            pl.BlockSpec((tm, A_WIDTH), tok),
            pl.BlockSpec((tm, A_WIDTH), tok),
            prev, nxt,
            pl.BlockSpec((None, None, 6, D), lambda i: (layer, mod_row(i), 0, 0)),
            _resident((None, len(POOL_WINDOWS), POOL_GROUP, POOL_GROUP), lambda i: (j, 0, 0, 0)),
            pl.BlockSpec(pool_scale.shape, const),
            _resident((None, D, D), lambda i: (j, 0, 0)),
        ],
        out_specs=pl.BlockSpec((tm, D), tok),
        out_shape=jax.ShapeDtypeStruct((n, D), F32),
        scratch_shapes=[pltpu.VMEM((tm + 2 * halo, A_WIDTH), F32)],
        compiler_params=_cparams(1),
        name="even_out",
    )(x, attn, p, p, p, mod4, pool_w, pool_scale, w_out)


def _conf_glu_kernel(x_ref, xp_ref, xn_ref, mod_ref, nmix_ref, w1_ref, gb_ref, h_ref, g_ref,
                     *, layer, tm, nsub, tiles_per_seq):
    g = nmix_ref[layer:layer + 1, :]
    sh = mod_ref[0:1, :]
    sc = mod_ref[1:2, :]
    halo = xp_ref.shape[0]
    sub_len = tm // nsub
    hm = _modnorm(x_ref[...], g, sh, sc)
    if nsub == 1:
        tis = pl.program_id(0) % tiles_per_seq
        hp = jnp.where(tis == 0, 0.0, _modnorm(xp_ref[...], g, sh, sc))
        hn = jnp.where(tis == tiles_per_seq - 1, 0.0, _modnorm(xn_ref[...], g, sh, sc))
        pieces = [hp, hm, hn]
    else:
        z = jnp.zeros((halo, D), F32)
        pieces = [z]
        for q in range(nsub):
            pieces += [hm[q * sub_len:(q + 1) * sub_len], z]
    h_ref[...] = jnp.concatenate(pieces, axis=0).astype(BF)

    seg = tm // CONV_SEGS
    seg_ext = seg + 2 * halo
    gp = _seg_pitch(seg_ext)
    segs_per_sub = CONV_SEGS // nsub
    halves = FF_CHUNK // LANES
    half_segs = CONV_SEGS // 2
    for c in range(D // FF_CHUNK):
        a = jnp.dot(h_ref[...], w1_ref[:, c * FF_CHUNK:(c + 1) * FF_CHUNK],
                    preferred_element_type=F32)
        gate = jnp.dot(h_ref[...], w1_ref[:, D + c * FF_CHUNK:D + (c + 1) * FF_CHUNK],
                       preferred_element_type=F32)
        glu = a * _sigmoid(gate)
        for hf in range(halves):
            slab = c * halves + hf
            for s in range(CONV_SEGS):
                start = s * seg + halo * (s // segs_per_sub)
                g_ref[slab, s * gp:s * gp + seg_ext, :] = (
                    glu[start:start + seg_ext, hf * LANES:(hf + 1) * LANES])
            for a0 in range(seg_ext):
                lo = g_ref[slab, pl.ds(a0, SUBLANES, stride=gp), :]
                hi = g_ref[slab, pl.ds(half_segs * gp + a0, SUBLANES, stride=gp), :]
                gb_ref[slab, a0 * CONV_SEGS:(a0 + 1) * CONV_SEGS, :] = (
                    jnp.concatenate([lo, hi], axis=0).astype(BF))


def _conf_conv_kernel(gb_ref, dw_ref, dwb_ref, u_ref, wb_ref, *, j, seg, off):
    @pl.when(pl.program_id(0) == 0)
    def _():
        for k in range(CONV_K):
            wk = jnp.broadcast_to(dw_ref[k:k + 1, :], (CONV_SEGS, D)).astype(BF)
            for slab in range(D // LANES):
                wb_ref[slab, k * CONV_SEGS:(k + 1) * CONV_SEGS, :] = wk[:, slab * LANES:(slab + 1) * LANES]

    for slab in range(D // LANES):
        bias = dwb_ref[j:j + 1, slab * LANES:(slab + 1) * LANES]
        for a0 in range(seg):
            acc = jnp.zeros((CONV_SEGS, LANES), F32)
            for k in range(CONV_K):
                t0 = (a0 + k + off) * CONV_SEGS
                acc = acc + (gb_ref[slab, t0:t0 + CONV_SEGS, :].astype(F32)
                             * wb_ref[slab, k * CONV_SEGS:(k + 1) * CONV_SEGS, :].astype(F32))
            u_ref[slab, a0 * CONV_SEGS:(a0 + 1) * CONV_SEGS, :] = (acc + bias).astype(u_ref.dtype)


def _conf_out_kernel(u_ref, x_ref, mod_ref, lng_ref, lnb_ref, w2_ref, o_ref, y_ref, *, j, seg):
    n_slabs = D // LANES
    tm = seg * CONV_SEGS
    pitch = _seg_pitch(CONV_SEGS)
    for r0 in range(0, tm, tm // 2):
        rows = slice(r0, r0 + tm // 2)
        u = jnp.concatenate([u_ref[slab, rows, :] for slab in range(n_slabs)], axis=-1).astype(F32)
        mu = jnp.mean(u, axis=-1, keepdims=True)
        uc = u - mu
        var = jnp.mean(uc * uc, axis=-1, keepdims=True)
        y = uc * lax.rsqrt(var + EPS) * lng_ref[j:j + 1, :] + lnb_ref[j:j + 1, :]
        y = y * _sigmoid(y)
        y = jnp.dot(y.astype(BF), w2_ref[...], preferred_element_type=F32)
        for a0 in range(r0 // CONV_SEGS, (r0 + tm // 2) // CONV_SEGS):
            blk = y[a0 * CONV_SEGS - r0:(a0 + 1) * CONV_SEGS - r0]
            for slab in range(n_slabs):
                y_ref[slab, a0 * pitch:a0 * pitch + CONV_SEGS, :] = blk[:, slab * LANES:(slab + 1) * LANES]
    gate_mix = mod_ref[2:3, :]
    for s in range(CONV_SEGS):
        ys = jnp.concatenate(
            [y_ref[slab, pl.ds(s, seg, stride=pitch), :] for slab in range(n_slabs)], axis=-1)
        rows = slice(s * seg, (s + 1) * seg)
        o_ref[rows, :] = x_ref[rows, :] + gate_mix * ys


def _seg_pitch(n):
    p = -(-n // 4)
    return 4 * (p if p % 2 else p + 1)


def _conformer(x, mod4, mod_row, norm_mix, w1, dw, dwb, lng, lnb, w2, *, layer, j, tm, seq):
    n = x.shape[0]
    tok = lambda i: (i, 0)
    const = lambda i: (0, 0)
    halo = 16
    prev, nxt = _halo_specs(D, tm, n, halo)
    nsub = max(tm // seq, 1)
    assert tm % seq == 0 or seq % tm == 0
    assert CONV_SEGS == 2 * SUBLANES and CONV_SEGS % nsub == 0 and halo >= CONV_K // 2
    seg = tm // CONV_SEGS
    seg_ext = seg + 2 * halo
    gp = _seg_pitch(seg_ext)
    ext_rows = tm + halo * (nsub + 1)
    n_tiles = n // tm
    n_slabs = D // LANES
    mod_spec = pl.BlockSpec((None, None, 6, D), lambda i: (layer, mod_row(i), 0, 0))
    gb_spec = pl.BlockSpec((None, n_slabs, seg_ext * CONV_SEGS, LANES), lambda i: (i, 0, 0, 0))
    u_spec = pl.BlockSpec((None, n_slabs, tm, LANES), lambda i: (i, 0, 0, 0))

    gb = pl.pallas_call(
        functools.partial(_conf_glu_kernel, layer=layer, tm=tm, nsub=nsub,
                          tiles_per_seq=max(seq // tm, 1)),
        grid=(n_tiles,),
        in_specs=[
            pl.BlockSpec((tm, D), tok), prev, nxt, mod_spec,
            pl.BlockSpec(norm_mix.shape, const),
            _resident((None, D, 2 * D), lambda i: (j, 0, 0)),
        ],
        out_specs=gb_spec,
        out_shape=jax.ShapeDtypeStruct((n_tiles, n_slabs, seg_ext * CONV_SEGS, LANES), BF),
        scratch_shapes=[
            pltpu.VMEM((ext_rows, D), BF),
            pltpu.VMEM((n_slabs, CONV_SEGS * gp, LANES), F32),
        ],
        compiler_params=_cparams(1),
        name="conf_glu",
    )(x, x, x, mod4, norm_mix, w1)

    u = pl.pallas_call(
        functools.partial(_conf_conv_kernel, j=j, seg=seg, off=halo - CONV_K // 2),
        grid=(n_tiles,),
        in_specs=[
            gb_spec,
            pl.BlockSpec((None, CONV_K, D), lambda i: (j, 0, 0)),
            pl.BlockSpec(dwb.shape, const),
        ],
        out_specs=u_spec,
        out_shape=jax.ShapeDtypeStruct((n_tiles, n_slabs, tm, LANES), BF),
        scratch_shapes=[pltpu.VMEM((n_slabs, CONV_K * CONV_SEGS, LANES), BF)],
        compiler_params=_cparams(1),
        name="conf_conv",
    )(gb, dw, dwb)

    return pl.pallas_call(
        functools.partial(_conf_out_kernel, j=j, seg=seg),
        grid=(n_tiles,),
        in_specs=[
            u_spec,
            pl.BlockSpec((tm, D), tok), mod_spec,
            pl.BlockSpec(lng.shape, const),
            pl.BlockSpec(lnb.shape, const),
            _resident((None, D, D), lambda i: (j, 0, 0)),
        ],
        out_specs=pl.BlockSpec((tm, D), tok),
        out_shape=jax.ShapeDtypeStruct((n, D), F32),
        scratch_shapes=[pltpu.VMEM((n_slabs, seg * _seg_pitch(CONV_SEGS), LANES), F32)],
        compiler_params=_cparams(1),
        name="conf_out",
    )(u, x, mod4, lng, lnb, w2)


def _ffn_kernel(x_ref, xp_ref, xn_ref, mod_ref, nffn_ref, wi_ref, cw_ref, cb_ref, wo_ref,
                o_ref, h_ref, act_ref, *, tm, nsub, tiles_per_seq, layer):
    g = nffn_ref[layer:layer + 1, :]
    sh = mod_ref[3:4, :]
    sc = mod_ref[4:5, :]
    halo = xp_ref.shape[0]
    sub_len = tm // nsub
    hm = _modnorm(x_ref[...], g, sh, sc)
    if nsub == 1:
        tis = pl.program_id(0) % tiles_per_seq
        hp = jnp.where(tis == 0, 0.0, _modnorm(xp_ref[...], g, sh, sc))
        hn = jnp.where(tis == tiles_per_seq - 1, 0.0, _modnorm(xn_ref[...], g, sh, sc))
        z = jnp.zeros((FFN_PAD - halo, D), F32)
        pieces = [z, hp, hm, hn, z]
    else:
        z = jnp.zeros((FFN_PAD, D), F32)
        pieces = [z]
        for q in range(nsub):
            pieces += [hm[q * sub_len:(q + 1) * sub_len], z]
    h_ref[...] = jnp.concatenate(pieces, axis=0).astype(BF)
    n_ext = tm + FFN_PAD * (nsub + 1)

    def conv(col):
        u = jnp.dot(h_ref[...], wi_ref[:, col:col + FF_CHUNK], preferred_element_type=F32)
        um = pltpu.roll(u, 1, 0)
        up = pltpu.roll(u, n_ext - 1, 0)
        return (cw_ref[0:1, col:col + FF_CHUNK] * um
                + cw_ref[1:2, col:col + FF_CHUNK] * u
                + cw_ref[2:3, col:col + FF_CHUNK] * up
                + cb_ref[layer:layer + 1, col:col + FF_CHUNK])

    for c in range(D_FF // FF_CHUNK):
        a = conv(c * FF_CHUNK)
        gate = conv(D_FF + c * FF_CHUNK)
        act = (a * _sigmoid(a) * gate).astype(BF)
        for q in range(nsub):
            r0 = FFN_PAD + q * (sub_len + FFN_PAD)
            act_ref[q * sub_len:(q + 1) * sub_len, c * FF_CHUNK:(c + 1) * FF_CHUNK] = act[r0:r0 + sub_len]
    y = jnp.dot(act_ref[...], wo_ref[...], preferred_element_type=F32)
    o_ref[...] = x_ref[...] + mod_ref[5:6, :] * y


def _ffn(x, mod4, mod_row, norm_ffn, w_in, conv_w, conv_b, w_out, *, layer, tm, seq):
    n = x.shape[0]
    tok = lambda i: (i, 0)
    const = lambda i: (0, 0)
    halo = 8
    prev, nxt = _halo_specs(D, tm, n, halo)
    nsub = max(tm // seq, 1)
    assert tm % seq == 0 or seq % tm == 0
    return pl.pallas_call(
        functools.partial(_ffn_kernel, layer=layer, tm=tm, nsub=nsub,
                          tiles_per_seq=max(seq // tm, 1)),
        grid=(n // tm,),
        in_specs=[
            pl.BlockSpec((tm, D), tok), prev, nxt,
            pl.BlockSpec((None, None, 6, D), lambda i: (layer, mod_row(i), 0, 0)),
            pl.BlockSpec(norm_ffn.shape, const),
            _resident((None, D, 2 * D_FF), lambda i: (layer, 0, 0)),
            pl.BlockSpec((None, 3, 2 * D_FF), lambda i: (layer, 0, 0)),
            pl.BlockSpec(conv_b.shape, const),
            _resident((None, D_FF, D), lambda i: (layer, 0, 0)),
        ],
        out_specs=pl.BlockSpec((tm, D), tok),
        out_shape=jax.ShapeDtypeStruct((n, D), F32),
        scratch_shapes=[
            pltpu.VMEM((tm + FFN_PAD * (nsub + 1), D), BF),
            pltpu.VMEM((tm, D_FF), BF),
        ],
        compiler_params=_cparams(1),
        name="conv_ffn",
    )(x, x, x, mod4, norm_ffn, w_in, conv_w, conv_b, w_out)


def kernel(x_prompt, x_sample, cache_k, cache_v, c, c_ctx, norm_mix, norm_ffn, w_mod, b_mod,
           w_in_ab, q_gain, k_gain, rpb, pool_w, pool_scale, w_out_ab,
           conv_pw1, conv_dw, conv_dw_b, conv_ln_g, conv_ln_b, conv_pw2,
           ffn_w_in, ffn_conv_w, ffn_conv_b, ffn_w_out):
    batch, seq, _ = x_prompt.shape
    dec_batch, dec_seq, _ = x_sample.shape
    n_even = w_in_ab.shape[0]
    past = cache_k.shape[2]

    xc = x_prompt.reshape(batch * seq, D)
    xl = x_sample.reshape(dec_batch * dec_seq, D)

    ctx_row = dec_batch
    c_all = jnp.concatenate([c, c_ctx[None, :], jnp.zeros((16 - dec_batch - 1, D), F32)], axis=0)
    mod4 = _modulation(c_all, w_mod, b_mod).reshape(DEPTH, 16, 6, D)
    bias = _bias_table(rpb, dec_seq // GRID_W)

    tm_c = seq
    tm_l = 1024
    tm_conf = 512
    tm_ffn = 1024
    row_c = lambda i: ctx_row
    row_l = lambda tm: (lambda i: i // (dec_seq // tm))

    head_ids = jnp.arange(A_WIDTH) // HEAD_DIM
    bd = (head_ids[:, None] == head_ids[None, :]).astype(BF)
    qg = jnp.tile(q_gain, (1, N_HEADS))
    kg = jnp.tile(k_gain, (1, N_HEADS))
    ck = cache_k.reshape(dec_batch, n_even, past, A_WIDTH)
    cv = cache_v.reshape(dec_batch, n_even, past, A_WIDTH)

    w_in_ab_b = w_in_ab.astype(BF)
    w_out_ab_b = w_out_ab.astype(BF)
    pool_w_b = pool_w.astype(BF)
    pw1_b = conv_pw1.astype(BF)
    pw2_b = conv_pw2.astype(BF)
    ffn_wi_b = ffn_w_in.astype(BF)
    ffn_wo_b = ffn_w_out.astype(BF)

    new_k, new_v = [], []
    for layer in range(DEPTH):
        j = layer // 2
        if layer % 2 == 0:
            qc, kc, vc, pc, kf, vf = _even_in(xc, mod4, row_c, norm_mix, w_in_ab_b, bd, qg, kg,
                                              layer=layer, j=j, tm=tm_c, emit_cache=True)
            ql, kl, vl, p_l = _even_in(xl, mod4, row_l(tm_l), norm_mix, w_in_ab_b, bd, qg, kg,
                                       layer=layer, j=j, tm=tm_l, emit_cache=False)
            ac = _ctx_attention(qc, kc, vc, seq)
            al = _lat_attention(ql, kl, vl, ck, cv, bias, j=j, batch=dec_batch, seq=dec_seq)
            xc = _even_out(xc, ac, pc, mod4, row_c, pool_w_b, pool_scale, w_out_ab_b,
                           layer=layer, j=j, tm=tm_c, seq=seq)
            xl = _even_out(xl, al, p_l, mod4, row_l(tm_l), pool_w_b, pool_scale, w_out_ab_b,
                           layer=layer, j=j, tm=tm_l, seq=dec_seq)
            new_k.append(kf.reshape(batch, seq, N_HEADS, HEAD_DIM))
            new_v.append(vf.reshape(batch, seq, N_HEADS, HEAD_DIM))
        else:
            xc = _conformer(xc, mod4, row_c, norm_mix, pw1_b, conv_dw, conv_dw_b, conv_ln_g,
                            conv_ln_b, pw2_b, layer=layer, j=j, tm=tm_conf, seq=seq)
            xl = _conformer(xl, mod4, row_l(tm_conf), norm_mix, pw1_b, conv_dw, conv_dw_b,
                            conv_ln_g, conv_ln_b, pw2_b, layer=layer, j=j, tm=tm_conf, seq=dec_seq)
        xc = _ffn(xc, mod4, row_c, norm_ffn, ffn_wi_b, ffn_conv_w, ffn_conv_b, ffn_wo_b,
                  layer=layer, tm=tm_ffn, seq=seq)
        xl = _ffn(xl, mod4, row_l(tm_ffn), norm_ffn, ffn_wi_b, ffn_conv_w, ffn_conv_b, ffn_wo_b,
                  layer=layer, tm=tm_ffn, seq=dec_seq)

    return (xc.reshape(batch, seq, D), xl.reshape(dec_batch, dec_seq, D),
            jnp.stack(new_k, axis=1), jnp.stack(new_v, axis=1))
```

```python
import functools

import jax
import jax.numpy as jnp
from jax import lax
from jax.experimental import pallas as pl
from jax.experimental.pallas import tpu as pltpu

D = 1024
DEPTH = 4
GRID_W = 64
A_WIDTH = 512
HEAD_DIM = 64
N_HEADS = 8
WIN_H = 8
WIN_W = 16
POOL_WINDOWS = (2, 4, 8, 16)
POOL_GROUP = 128
CONV_K = 31
D_FF = 2816
EPS = 1e-6
NEG_INF = -1e30

F32 = jnp.float32
BF = jnp.bfloat16

LANES = 128
SUBLANES = 8
CONV_SEGS = 16
FFN_PAD = 16
FF_CHUNK = 256
VMEM_LIMIT = 56 * 1024 * 1024


def _cparams(n_grid):
    return pltpu.CompilerParams(
        dimension_semantics=("arbitrary",) * n_grid, vmem_limit_bytes=VMEM_LIMIT)


def _resident(shape, index_map):
    return pl.BlockSpec(shape, index_map, pipeline_mode=pl.Buffered(1))


def _sigmoid(x):
    return 1.0 / (1.0 + jnp.exp(-x))


def _modnorm(x, g, shift, scale):
    ms = jnp.mean(x * x, axis=-1, keepdims=True)
    return (x * lax.rsqrt(ms + EPS) * g) * (1.0 + scale) + shift


def _mod_kernel(c_ref, w_ref, b_ref, o_ref):
    c = c_ref[...]
    s = c * _sigmoid(c)
    o_ref[...] = jnp.dot(s.astype(BF), w_ref[...].astype(BF),
                         preferred_element_type=F32) + b_ref[...]


def _modulation(c_all, w_mod, b_mod):
    tn = 1536
    return pl.pallas_call(
        _mod_kernel,
        grid=(DEPTH, 6 * D // tn),
        in_specs=[
            pl.BlockSpec((16, D), lambda l, n: (0, 0)),
            pl.BlockSpec((None, D, tn), lambda l, n: (l, 0, n)),
            pl.BlockSpec((None, 1, tn), lambda l, n: (l, 0, n)),
        ],
        out_specs=pl.BlockSpec((None, 16, tn), lambda l, n: (l, 0, n)),
        out_shape=jax.ShapeDtypeStruct((DEPTH, 16, 6 * D), F32),
        compiler_params=_cparams(2),
        name="modulation",
    )(c_all, w_mod, b_mod.reshape(DEPTH, 1, 6 * D))


Q_ROWS = WIN_H // 2
WIN_ROWS = Q_ROWS + WIN_H - 1
N_BIAS_CLASSES = 3


def _group_window_start(r0, rows):
    return jnp.clip(r0 - WIN_H // 2, 0, rows - WIN_ROWS)


def _bias_kernel(b_ref, o_ref, t_ref, *, rows):
    cls = pl.program_id(1)
    q = lax.broadcasted_iota(jnp.int32, (GRID_W, LANES), 0)
    l = lax.broadcasted_iota(jnp.int32, (GRID_W, LANES), 1)
    n_rho = 2 * WIN_H - 1
    lo = l < GRID_W

    @pl.when(cls == 0)
    def _():
        ck = l & (GRID_W - 1)
        cs = jnp.clip(q - WIN_W // 2, 0, GRID_W - WIN_W)
        valid = (ck >= cs) & (ck < cs + WIN_W)

        def body(hr, carry):
            row = jnp.broadcast_to(b_ref[pl.ds(hr, 1), :], (GRID_W, LANES))
            r_lo = pltpu.roll(row, 0, 1, stride=1, stride_axis=0)
            r_hi = pltpu.roll(row, GRID_W, 1, stride=1, stride_axis=0)
            t_ref[hr] = jnp.where(valid, jnp.where(lo, r_lo, r_hi), NEG_INF)
            return carry

        lax.fori_loop(0, N_HEADS * n_rho, body, 0, unroll=8)

    r0 = jnp.where(cls == 0, 0, jnp.where(cls == 1, Q_ROWS, rows - Q_ROWS))
    ws = _group_window_start(r0, rows)
    neg = jnp.full((GRID_W, LANES), NEG_INF, F32)
    for h in range(N_HEADS):
        for rq in range(Q_ROWS):
            r = r0 + rq
            rs = jnp.clip(r - WIN_H // 2, 0, rows - WIN_H)

            def blk(kappa):
                ka = ws + kappa
                ok = (ka >= rs) & (ka < rs + WIN_H)
                rho = jnp.clip(ka - r + (WIN_H - 1), 0, n_rho - 1)
                return jnp.where(ok, t_ref[h * n_rho + rho], neg)

            qs = slice(rq * GRID_W, (rq + 1) * GRID_W)
            for pr in range(WIN_ROWS // 2):
                o_ref[h, qs, pr * LANES:(pr + 1) * LANES] = jnp.where(lo, blk(2 * pr), blk(2 * pr + 1))
            if WIN_ROWS % 2:
                tail = (WIN_ROWS - 1) * GRID_W
                o_ref[h, qs, tail:tail + GRID_W] = blk(WIN_ROWS - 1)[:, 0:GRID_W]


def _bias_table(rpb, rows):
    n_even = rpb.shape[0]
    blk = (N_HEADS, Q_ROWS * GRID_W, WIN_ROWS * GRID_W)
    n_hr = N_HEADS * (2 * WIN_H - 1)
    n_d = 2 * WIN_W - 1
    rp = rpb.reshape(n_even, n_hr, n_d)
    b_rows = jnp.concatenate(
        [rp[..., WIN_W - 1:], jnp.zeros((n_even, n_hr, LANES - n_d), F32), rp[..., :WIN_W - 1]], axis=-1)
    return pl.pallas_call(
        functools.partial(_bias_kernel, rows=rows),
        grid=(n_even, N_BIAS_CLASSES),
        in_specs=[pl.BlockSpec((None, n_hr, LANES), lambda j, c: (j, 0, 0))],
        out_specs=pl.BlockSpec((None, None) + blk, lambda j, c: (j, c, 0, 0, 0)),
        out_shape=jax.ShapeDtypeStruct((n_even, N_BIAS_CLASSES) + blk, F32),
        scratch_shapes=[pltpu.VMEM((N_HEADS * (2 * WIN_H - 1), GRID_W, LANES), F32)],
        compiler_params=_cparams(2),
        name="bias_table",
    )(b_rows)


def _even_in_kernel(x_ref, mod_ref, nmix_ref, w_ref, bd_ref, qg_ref, kg_ref,
                    q_ref, k_ref, v_ref, p_ref, *cache_refs, layer, j):
    g = nmix_ref[layer:layer + 1, :]
    h = _modnorm(x_ref[...], g, mod_ref[0:1, :], mod_ref[1:2, :]).astype(BF)
    proj = jnp.dot(h, w_ref[...], preferred_element_type=F32)

    def headnorm(t, gain):
        ss = jnp.dot((t * t).astype(BF), bd_ref[...], preferred_element_type=F32)
        return t * lax.rsqrt(ss * (1.0 / HEAD_DIM) + EPS) * gain

    q = headnorm(proj[:, 0:A_WIDTH], qg_ref[j:j + 1, :])
    k = headnorm(proj[:, A_WIDTH:2 * A_WIDTH], kg_ref[j:j + 1, :])
    v = proj[:, 2 * A_WIDTH:3 * A_WIDTH]
    q_ref[...] = (q * (HEAD_DIM ** -0.5)).astype(BF)
    k_ref[...] = k.astype(BF)
    v_ref[...] = v.astype(BF)
    p_ref[...] = proj[:, 3 * A_WIDTH:]
    if cache_refs:
        cache_refs[0][...] = k
        cache_refs[1][...] = v


def _even_in(x, mod4, mod_row, norm_mix, w_in, bd, qg, kg, *, layer, j, tm, emit_cache):
    n = x.shape[0]
    tok = lambda i: (i, 0)
    const = lambda i: (0, 0)
    out_shape = [jax.ShapeDtypeStruct((n, A_WIDTH), BF)] * 3 + [jax.ShapeDtypeStruct((n, A_WIDTH), F32)]
    out_specs = [pl.BlockSpec((tm, A_WIDTH), tok)] * 4
    if emit_cache:
        out_shape += [jax.ShapeDtypeStruct((n, A_WIDTH), F32)] * 2
        out_specs += [pl.BlockSpec((tm, A_WIDTH), tok)] * 2
    return pl.pallas_call(
        functools.partial(_even_in_kernel, layer=layer, j=j),
        grid=(n // tm,),
        in_specs=[
            pl.BlockSpec((tm, D), tok),
            pl.BlockSpec((None, None, 6, D), lambda i: (layer, mod_row(i), 0, 0)),
            pl.BlockSpec(norm_mix.shape, const),
            _resident((None, D, 4 * A_WIDTH), lambda i: (j, 0, 0)),
            _resident((A_WIDTH, A_WIDTH), const),
            pl.BlockSpec(qg.shape, const),
            pl.BlockSpec(kg.shape, const),
        ],
        out_specs=out_specs,
        out_shape=out_shape,
        compiler_params=_cparams(1),
        name="even_in",
    )(x, mod4, norm_mix, w_in, bd, qg, kg)


def _dot_nt(a, b):
    return lax.dot_general(a, b, (((1,), (1,)), ((), ())), preferred_element_type=F32)


def _ctx_attn_kernel(q_ref, k_ref, v_ref, o_ref):
    tq = q_ref.shape[0]
    lo = lax.broadcasted_iota(jnp.int32, (1, LANES), 1) < HEAD_DIM
    for pr in range(N_HEADS // 2):
        sl = slice(pr * LANES, (pr + 1) * LANES)
        q2 = q_ref[:, sl]
        k2 = k_ref[:, sl]
        v2 = v_ref[:, sl]
        zero = jnp.zeros_like(q2)
        qm = jnp.concatenate([jnp.where(lo, q2, zero), jnp.where(lo, zero, q2)], axis=0)
        s = _dot_nt(qm, k2)
        m = jnp.max(s, axis=-1, keepdims=True)
        pe = jnp.exp(s - m)
        den = jnp.sum(pe, axis=-1, keepdims=True)
        o = jnp.dot(pe.astype(BF), v2, preferred_element_type=F32) / den
        o_ref[:, sl] = jnp.where(lo, o[0:tq], o[tq:2 * tq]).astype(o_ref.dtype)


def _ctx_attention(q, k, v, seq):
    n = q.shape[0]
    spec = pl.BlockSpec((seq, A_WIDTH), lambda b: (b, 0))
    return pl.pallas_call(
        _ctx_attn_kernel,
        grid=(n // seq,),
        in_specs=[spec, spec, spec],
        out_specs=spec,
        out_shape=jax.ShapeDtypeStruct((n, A_WIDTH), BF),
        compiler_params=_cparams(1),
        name="ctx_attention",
    )(q, k, v)


def _lat_attn_kernel(q_ref, k_ref, v_ref, ck_ref, cv_ref, b_ref, o_ref, *, rows):
    ws = _group_window_start(pl.program_id(1) * Q_ROWS, rows)
    k0 = pl.multiple_of(ws * GRID_W, GRID_W)
    n_loc = WIN_ROWS * GRID_W
    tq = q_ref.shape[0]
    lo = lax.broadcasted_iota(jnp.int32, (1, LANES), 1) < HEAD_DIM
    for pr in range(N_HEADS // 2):
        sl = slice(pr * LANES, (pr + 1) * LANES)
        q2 = q_ref[:, sl]
        k2 = k_ref[pl.ds(k0, n_loc), sl]
        v2 = v_ref[pl.ds(k0, n_loc), sl]
        ck2 = ck_ref[:, sl].astype(BF)
        cv2 = cv_ref[:, sl].astype(BF)
        zero = jnp.zeros_like(q2)
        qm = jnp.concatenate([jnp.where(lo, q2, zero), jnp.where(lo, zero, q2)], axis=0)
        s_loc = _dot_nt(qm, k2) + b_ref[2 * pr:2 * pr + 2].reshape(2 * tq, n_loc)
        s_ctx = _dot_nt(qm, ck2)
        m = jnp.maximum(jnp.max(s_loc, axis=-1, keepdims=True),
                        jnp.max(s_ctx, axis=-1, keepdims=True))
        p_loc = jnp.exp(s_loc - m)
        p_ctx = jnp.exp(s_ctx - m)
        den = jnp.sum(p_loc, axis=-1, keepdims=True) + jnp.sum(p_ctx, axis=-1, keepdims=True)
        o = (jnp.dot(p_loc.astype(BF), v2, preferred_element_type=F32)
             + jnp.dot(p_ctx.astype(BF), cv2, preferred_element_type=F32)) / den
        o_ref[:, sl] = jnp.where(lo, o[0:tq], o[tq:2 * tq]).astype(o_ref.dtype)


def _lat_attention(q, k, v, cache_k, cache_v, bias, *, j, batch, seq):
    rows = seq // GRID_W
    groups = rows // Q_ROWS
    assert rows % Q_ROWS == 0 and groups >= 3 and rows >= WIN_ROWS
    past = cache_k.shape[2]
    tq = Q_ROWS * GRID_W

    def bias_class(g):
        return jnp.where(g == 0, 0, jnp.where(g == groups - 1, 2, 1))

    kv_spec = pl.BlockSpec((seq, A_WIDTH), lambda b, g: (b, 0))
    cache_spec = pl.BlockSpec((None, None, past, A_WIDTH), lambda b, g: (b, j, 0, 0))
    return pl.pallas_call(
        functools.partial(_lat_attn_kernel, rows=rows),
        grid=(batch, groups),
        in_specs=[
            pl.BlockSpec((tq, A_WIDTH), lambda b, g: (b * groups + g, 0)),
            kv_spec, kv_spec, cache_spec, cache_spec,
            pl.BlockSpec((None, None, N_HEADS, tq, WIN_ROWS * GRID_W),
                         lambda b, g: (j, bias_class(g), 0, 0, 0)),
        ],
        out_specs=pl.BlockSpec((tq, A_WIDTH), lambda b, g: (b * groups + g, 0)),
        out_shape=jax.ShapeDtypeStruct((batch * seq, A_WIDTH), BF),
        compiler_params=_cparams(2),
        name="lat_attention",
    )(q, k, v, cache_k, cache_v, bias)


def _even_out_kernel(x_ref, a_ref, p_ref, pp_ref, pn_ref, mod_ref, pw_ref, ps_ref, wo_ref,
                     o_ref, ext_ref, *, j, tm, tiles_per_seq, seq):
    tis = pl.program_id(0) % tiles_per_seq
    halo = pp_ref.shape[0]
    ext_ref[0:halo, :] = jnp.where(tis == 0, 0.0, pp_ref[...])
    ext_ref[halo:halo + tm, :] = p_ref[...]
    ext_ref[halo + tm:2 * halo + tm, :] = jnp.where(tis == tiles_per_seq - 1, 0.0, pn_ref[...])
    t = tis * tm + lax.broadcasted_iota(jnp.int32, (tm, 1), 0)
    pooled = []
    for g, w in enumerate(POOL_WINDOWS):
        sl = slice(g * POOL_GROUP, (g + 1) * POOL_GROUP)
        acc = None
        for off in range(-(w // 2), w // 2):
            term = ext_ref[halo + off:halo + off + tm, sl]
            acc = term if acc is None else acc + term
        cnt = (jnp.minimum(t + (w // 2 - 1), seq - 1) - jnp.maximum(t - w // 2, 0) + 1).astype(F32)
        d = acc / cnt - p_ref[:, sl]
        y = jnp.dot(d.astype(BF), pw_ref[g], preferred_element_type=F32)
        pooled.append(y * ps_ref[j:j + 1, sl])
    pooled = jnp.concatenate(pooled, axis=-1).astype(BF)
    y = (jnp.dot(a_ref[...], wo_ref[0:A_WIDTH, :], preferred_element_type=F32)
         + jnp.dot(pooled, wo_ref[A_WIDTH:, :], preferred_element_type=F32))
    o_ref[...] = x_ref[...] + mod_ref[2:3, :] * y


def _halo_specs(width, tm, n, halo):
    blocks = tm // halo
    last = n // halo - 1
    prev = pl.BlockSpec((halo, width), lambda i: (jnp.maximum(i * blocks - 1, 0), 0))
    nxt = pl.BlockSpec((halo, width), lambda i: (jnp.minimum((i + 1) * blocks, last), 0))
    return prev, nxt


def _even_out(x, attn, p, mod4, mod_row, pool_w, pool_scale, w_out, *, layer, j, tm, seq):
    n = x.shape[0]
    tok = lambda i: (i, 0)
    const = lambda i: (0, 0)
    halo = 8
    prev, nxt = _halo_specs(A_WIDTH, tm, n, halo)
    return pl.pallas_call(
        functools.partial(_even_out_kernel, j=j, tm=tm, tiles_per_seq=seq // tm, seq=seq),
        grid=(n // tm,),
        in_specs=[
            pl.BlockSpec((tm, D), tok),
            pl.BlockSpec((tm, A_WIDTH), tok),
            pl.BlockSpec((tm, A_WIDTH), tok),
            prev, nxt,
            pl.BlockSpec((None, None, 6, D), lambda i: (layer, mod_row(i), 0, 0)),
            _resident((None, len(POOL_WINDOWS), POOL_GROUP, POOL_GROUP), lambda i: (j, 0, 0, 0)),
            pl.BlockSpec(pool_scale.shape, const),
            _resident((None, D, D), lambda i: (j, 0, 0)),
        ],
        out_specs=pl.BlockSpec((tm, D), tok),
        out_shape=jax.ShapeDtypeStruct((n, D), F32),
        scratch_shapes=[pltpu.VMEM((tm + 2 * halo, A_WIDTH), F32)],
        compiler_params=_cparams(1),
        name="even_out",
    )(x, attn, p, p, p, mod4, pool_w, pool_scale, w_out)


def _conf_glu_kernel(x_ref, xp_ref, xn_ref, mod_ref, nmix_ref, w1_ref, gb_ref, h_ref, g_ref,
                     *, layer, tm, nsub, tiles_per_seq):
    g = nmix_ref[layer:layer + 1, :]
    sh = mod_ref[0:1, :]
    sc = mod_ref[1:2, :]
    halo = xp_ref.shape[0]
    sub_len = tm // nsub
    hm = _modnorm(x_ref[...], g, sh, sc)
    if nsub == 1:
        tis = pl.program_id(0) % tiles_per_seq
        hp = jnp.where(tis == 0, 0.0, _modnorm(xp_ref[...], g, sh, sc))
        hn = jnp.where(tis == tiles_per_seq - 1, 0.0, _modnorm(xn_ref[...], g, sh, sc))
        pieces = [hp, hm, hn]
    else:
        z = jnp.zeros((halo, D), F32)
        pieces = [z]
        for q in range(nsub):
            pieces += [hm[q * sub_len:(q + 1) * sub_len], z]
    h_ref[...] = jnp.concatenate(pieces, axis=0).astype(BF)

    seg = tm // CONV_SEGS
    seg_ext = seg + 2 * halo
    gp = _seg_pitch(seg_ext)
    segs_per_sub = CONV_SEGS // nsub
    halves = FF_CHUNK // LANES
    half_segs = CONV_SEGS // 2
    for c in range(D // FF_CHUNK):
        a = jnp.dot(h_ref[...], w1_ref[:, c * FF_CHUNK:(c + 1) * FF_CHUNK],
                    preferred_element_type=F32)
        gate = jnp.dot(h_ref[...], w1_ref[:, D + c * FF_CHUNK:D + (c + 1) * FF_CHUNK],
                       preferred_element_type=F32)
        glu = a * _sigmoid(gate)
        for hf in range(halves):
            slab = c * halves + hf
            for s in range(CONV_SEGS):
                start = s * seg + halo * (s // segs_per_sub)
                g_ref[slab, s * gp:s * gp + seg_ext, :] = (
                    glu[start:start + seg_ext, hf * LANES:(hf + 1) * LANES])
            for a0 in range(seg_ext):
                lo = g_ref[slab, pl.ds(a0, SUBLANES, stride=gp), :]
                hi = g_ref[slab, pl.ds(half_segs * gp + a0, SUBLANES, stride=gp), :]
                gb_ref[slab, a0 * CONV_SEGS:(a0 + 1) * CONV_SEGS, :] = (
                    jnp.concatenate([lo, hi], axis=0).astype(BF))


def _conf_conv_kernel(gb_ref, dw_ref, dwb_ref, u_ref, wb_ref, *, j, seg, off):
    @pl.when(pl.program_id(0) == 0)
    def _():
        for k in range(CONV_K):
            wk = jnp.broadcast_to(dw_ref[k:k + 1, :], (CONV_SEGS, D)).astype(BF)
            for slab in range(D // LANES):
                wb_ref[slab, k * CONV_SEGS:(k + 1) * CONV_SEGS, :] = wk[:, slab * LANES:(slab + 1) * LANES]

    for slab in range(D // LANES):
        bias = dwb_ref[j:j + 1, slab * LANES:(slab + 1) * LANES]
        for a0 in range(seg):
            acc = jnp.zeros((CONV_SEGS, LANES), F32)
            for k in range(CONV_K):
                t0 = (a0 + k + off) * CONV_SEGS
                acc = acc + (gb_ref[slab, t0:t0 + CONV_SEGS, :].astype(F32)
                             * wb_ref[slab, k * CONV_SEGS:(k + 1) * CONV_SEGS, :].astype(F32))
            u_ref[slab, a0 * CONV_SEGS:(a0 + 1) * CONV_SEGS, :] = (acc + bias).astype(u_ref.dtype)


def _conf_out_kernel(u_ref, x_ref, mod_ref, lng_ref, lnb_ref, w2_ref, o_ref, y_ref, *, j, seg):
    n_slabs = D // LANES
    tm = seg * CONV_SEGS
    pitch = _seg_pitch(CONV_SEGS)
    for r0 in range(0, tm, tm // 2):
        rows = slice(r0, r0 + tm // 2)
        u = jnp.concatenate([u_ref[slab, rows, :] for slab in range(n_slabs)], axis=-1).astype(F32)
        mu = jnp.mean(u, axis=-1, keepdims=True)
        uc = u - mu
        var = jnp.mean(uc * uc, axis=-1, keepdims=True)
        y = uc * lax.rsqrt(var + EPS) * lng_ref[j:j + 1, :] + lnb_ref[j:j + 1, :]
        y = y * _sigmoid(y)
        y = jnp.dot(y.astype(BF), w2_ref[...], preferred_element_type=F32)
        for a0 in range(r0 // CONV_SEGS, (r0 + tm // 2) // CONV_SEGS):
            blk = y[a0 * CONV_SEGS - r0:(a0 + 1) * CONV_SEGS - r0]
            for slab in range(n_slabs):
                y_ref[slab, a0 * pitch:a0 * pitch + CONV_SEGS, :] = blk[:, slab * LANES:(slab + 1) * LANES]
    gate_mix = mod_ref[2:3, :]
    for s in range(CONV_SEGS):
        ys = jnp.concatenate(
            [y_ref[slab, pl.ds(s, seg, stride=pitch), :] for slab in range(n_slabs)], axis=-1)
        rows = slice(s * seg, (s + 1) * seg)
        o_ref[rows, :] = x_ref[rows, :] + gate_mix * ys


def _seg_pitch(n):
    p = -(-n // 4)
    return 4 * (p if p % 2 else p + 1)


def _conformer(x, mod4, mod_row, norm_mix, w1, dw, dwb, lng, lnb, w2, *, layer, j, tm, seq):
    n = x.shape[0]
    tok = lambda i: (i, 0)
    const = lambda i: (0, 0)
    halo = 16
    prev, nxt = _halo_specs(D, tm, n, halo)
    nsub = max(tm // seq, 1)
    assert tm % seq == 0 or seq % tm == 0
    assert CONV_SEGS == 2 * SUBLANES and CONV_SEGS % nsub == 0 and halo >= CONV_K // 2
    seg = tm // CONV_SEGS
    seg_ext = seg + 2 * halo
    gp = _seg_pitch(seg_ext)
    ext_rows = tm + halo * (nsub + 1)
    n_tiles = n // tm
    n_slabs = D // LANES
    mod_spec = pl.BlockSpec((None, None, 6, D), lambda i: (layer, mod_row(i), 0, 0))
    gb_spec = pl.BlockSpec((None, n_slabs, seg_ext * CONV_SEGS, LANES), lambda i: (i, 0, 0, 0))
    u_spec = pl.BlockSpec((None, n_slabs, tm, LANES), lambda i: (i, 0, 0, 0))

    gb = pl.pallas_call(
        functools.partial(_conf_glu_kernel, layer=layer, tm=tm, nsub=nsub,
                          tiles_per_seq=max(seq // tm, 1)),
        grid=(n_tiles,),
        in_specs=[
            pl.BlockSpec((tm, D), tok), prev, nxt, mod_spec,
            pl.BlockSpec(norm_mix.shape, const),
            _resident((None, D, 2 * D), lambda i: (j, 0, 0)),
        ],
        out_specs=gb_spec,
        out_shape=jax.ShapeDtypeStruct((n_tiles, n_slabs, seg_ext * CONV_SEGS, LANES), BF),
        scratch_shapes=[
            pltpu.VMEM((ext_rows, D), BF),
            pltpu.VMEM((n_slabs, CONV_SEGS * gp, LANES), F32),
        ],
        compiler_params=_cparams(1),
        name="conf_glu",
    )(x, x, x, mod4, norm_mix, w1)

    u = pl.pallas_call(
        functools.partial(_conf_conv_kernel, j=j, seg=seg, off=halo - CONV_K // 2),
        grid=(n_tiles,),
        in_specs=[
            gb_spec,
            pl.BlockSpec((None, CONV_K, D), lambda i: (j, 0, 0)),
            pl.BlockSpec(dwb.shape, const),
        ],
        out_specs=u_spec,
        out_shape=jax.ShapeDtypeStruct((n_tiles, n_slabs, tm, LANES), BF),
        scratch_shapes=[pltpu.VMEM((n_slabs, CONV_K * CONV_SEGS, LANES), BF)],
        compiler_params=_cparams(1),
        name="conf_conv",
    )(gb, dw, dwb)

    return pl.pallas_call(
        functools.partial(_conf_out_kernel, j=j, seg=seg),
        grid=(n_tiles,),
        in_specs=[
            u_spec,
            pl.BlockSpec((tm, D), tok), mod_spec,
            pl.BlockSpec(lng.shape, const),
            pl.BlockSpec(lnb.shape, const),
            _resident((None, D, D), lambda i: (j, 0, 0)),
        ],
        out_specs=pl.BlockSpec((tm, D), tok),
        out_shape=jax.ShapeDtypeStruct((n, D), F32),
        scratch_shapes=[pltpu.VMEM((n_slabs, seg * _seg_pitch(CONV_SEGS), LANES), F32)],
        compiler_params=_cparams(1),
        name="conf_out",
    )(u, x, mod4, lng, lnb, w2)


def _ffn_kernel(x_ref, xp_ref, xn_ref, mod_ref, nffn_ref, wi_ref, cw_ref, cb_ref, wo_ref,
                o_ref, h_ref, act_ref, *, tm, nsub, tiles_per_seq, layer):
    g = nffn_ref[layer:layer + 1, :]
    sh = mod_ref[3:4, :]
    sc = mod_ref[4:5, :]
    halo = xp_ref.shape[0]
    sub_len = tm // nsub
    hm = _modnorm(x_ref[...], g, sh, sc)
    if nsub == 1:
        tis = pl.program_id(0) % tiles_per_seq
        hp = jnp.where(tis == 0, 0.0, _modnorm(xp_ref[...], g, sh, sc))
        hn = jnp.where(tis == tiles_per_seq - 1, 0.0, _modnorm(xn_ref[...], g, sh, sc))
        z = jnp.zeros((FFN_PAD - halo, D), F32)
        pieces = [z, hp, hm, hn, z]
    else:
        z = jnp.zeros((FFN_PAD, D), F32)
        pieces = [z]
        for q in range(nsub):
            pieces += [hm[q * sub_len:(q + 1) * sub_len], z]
    h_ref[...] = jnp.concatenate(pieces, axis=0).astype(BF)
    n_ext = tm + FFN_PAD * (nsub + 1)

    def conv(col):
        u = jnp.dot(h_ref[...], wi_ref[:, col:col + FF_CHUNK], preferred_element_type=F32)
        um = pltpu.roll(u, 1, 0)
        up = pltpu.roll(u, n_ext - 1, 0)
        return (cw_ref[0:1, col:col + FF_CHUNK] * um
                + cw_ref[1:2, col:col + FF_CHUNK] * u
                + cw_ref[2:3, col:col + FF_CHUNK] * up
                + cb_ref[layer:layer + 1, col:col + FF_CHUNK])

    for c in range(D_FF // FF_CHUNK):
        a = conv(c * FF_CHUNK)
        gate = conv(D_FF + c * FF_CHUNK)
        act = (a * _sigmoid(a) * gate).astype(BF)
        for q in range(nsub):
            r0 = FFN_PAD + q * (sub_len + FFN_PAD)
            act_ref[q * sub_len:(q + 1) * sub_len, c * FF_CHUNK:(c + 1) * FF_CHUNK] = act[r0:r0 + sub_len]
    y = jnp.dot(act_ref[...], wo_ref[...], preferred_element_type=F32)
    o_ref[...] = x_ref[...] + mod_ref[5:6, :] * y


def _ffn(x, mod4, mod_row, norm_ffn, w_in, conv_w, conv_b, w_out, *, layer, tm, seq):
    n = x.shape[0]
    tok = lambda i: (i, 0)
    const = lambda i: (0, 0)
    halo = 8
    prev, nxt = _halo_specs(D, tm, n, halo)
    nsub = max(tm // seq, 1)
    assert tm % seq == 0 or seq % tm == 0
    return pl.pallas_call(
        functools.partial(_ffn_kernel, layer=layer, tm=tm, nsub=nsub,
                          tiles_per_seq=max(seq // tm, 1)),
        grid=(n // tm,),
        in_specs=[
            pl.BlockSpec((tm, D), tok), prev, nxt,
            pl.BlockSpec((None, None, 6, D), lambda i: (layer, mod_row(i), 0, 0)),
            pl.BlockSpec(norm_ffn.shape, const),
            _resident((None, D, 2 * D_FF), lambda i: (layer, 0, 0)),
            pl.BlockSpec((None, 3, 2 * D_FF), lambda i: (layer, 0, 0)),
            pl.BlockSpec(conv_b.shape, const),
            _resident((None, D_FF, D), lambda i: (layer, 0, 0)),
        ],
        out_specs=pl.BlockSpec((tm, D), tok),
        out_shape=jax.ShapeDtypeStruct((n, D), F32),
        scratch_shapes=[
            pltpu.VMEM((tm + FFN_PAD * (nsub + 1), D), BF),
            pltpu.VMEM((tm, D_FF), BF),
        ],
        compiler_params=_cparams(1),
        name="conv_ffn",
    )(x, x, x, mod4, norm_ffn, w_in, conv_w, conv_b, w_out)


def kernel(x_prompt, x_sample, cache_k, cache_v, c, c_ctx, norm_mix, norm_ffn, w_mod, b_mod,
           w_in_ab, q_gain, k_gain, rpb, pool_w, pool_scale, w_out_ab,
           conv_pw1, conv_dw, conv_dw_b, conv_ln_g, conv_ln_b, conv_pw2,
           ffn_w_in, ffn_conv_w, ffn_conv_b, ffn_w_out):
    batch, seq, _ = x_prompt.shape
    dec_batch, dec_seq, _ = x_sample.shape
    n_even = w_in_ab.shape[0]
    past = cache_k.shape[2]

    xc = x_prompt.reshape(batch * seq, D)
    xl = x_sample.reshape(dec_batch * dec_seq, D)

    ctx_row = dec_batch
    c_all = jnp.concatenate([c, c_ctx[None, :], jnp.zeros((16 - dec_batch - 1, D), F32)], axis=0)
    mod4 = _modulation(c_all, w_mod, b_mod).reshape(DEPTH, 16, 6, D)
    bias = _bias_table(rpb, dec_seq // GRID_W)

    tm_c = seq
    tm_l = 1024
    tm_conf = 512
    tm_ffn = 1024
    row_c = lambda i: ctx_row
    row_l = lambda tm: (lambda i: i // (dec_seq // tm))

    head_ids = jnp.arange(A_WIDTH) // HEAD_DIM
    bd = (head_ids[:, None] == head_ids[None, :]).astype(BF)
    qg = jnp.tile(q_gain, (1, N_HEADS))
    kg = jnp.tile(k_gain, (1, N_HEADS))
    ck = cache_k.reshape(dec_batch, n_even, past, A_WIDTH)
    cv = cache_v.reshape(dec_batch, n_even, past, A_WIDTH)

    w_in_ab_b = w_in_ab.astype(BF)
    w_out_ab_b = w_out_ab.astype(BF)
    pool_w_b = pool_w.astype(BF)
    pw1_b = conv_pw1.astype(BF)
    pw2_b = conv_pw2.astype(BF)
    ffn_wi_b = ffn_w_in.astype(BF)
    ffn_wo_b = ffn_w_out.astype(BF)

    new_k, new_v = [], []
    for layer in range(DEPTH):
        j = layer // 2
        if layer % 2 == 0:
            qc, kc, vc, pc, kf, vf = _even_in(xc, mod4, row_c, norm_mix, w_in_ab_b, bd, qg, kg,
                                              layer=layer, j=j, tm=tm_c, emit_cache=True)
            ql, kl, vl, p_l = _even_in(xl, mod4, row_l(tm_l), norm_mix, w_in_ab_b, bd, qg, kg,
                                       layer=layer, j=j, tm=tm_l, emit_cache=False)
            ac = _ctx_attention(qc, kc, vc, seq)
            al = _lat_attention(ql, kl, vl, ck, cv, bias, j=j, batch=dec_batch, seq=dec_seq)
            xc = _even_out(xc, ac, pc, mod4, row_c, pool_w_b, pool_scale, w_out_ab_b,
                           layer=layer, j=j, tm=tm_c, seq=seq)
            xl = _even_out(xl, al, p_l, mod4, row_l(tm_l), pool_w_b, pool_scale, w_out_ab_b,
                           layer=layer, j=j, tm=tm_l, seq=dec_seq)
            new_k.append(kf.reshape(batch, seq, N_HEADS, HEAD_DIM))
            new_v.append(vf.reshape(batch, seq, N_HEADS, HEAD_DIM))
        else:
            xc = _conformer(xc, mod4, row_c, norm_mix, pw1_b, conv_dw, conv_dw_b, conv_ln_g,
                            conv_ln_b, pw2_b, layer=layer, j=j, tm=tm_conf, seq=seq)
            xl = _conformer(xl, mod4, row_l(tm_ffn), norm_mix, pw1_b, conv_dw, conv_dw_b,
                            conv_ln_g, conv_ln_b, pw2_b, layer=layer, j=j, tm=tm_ffn, seq=dec_seq)
        xc = _ffn(xc, mod4, row_c, norm_ffn, ffn_wi_b, ffn_conv_w, ffn_conv_b, ffn_wo_b,
                  layer=layer, tm=tm_ffn, seq=seq)
        xl = _ffn(xl, mod4, row_l(tm_ffn), norm_ffn, ffn_wi_b, ffn_conv_w, ffn_conv_b, ffn_wo_b,
                  layer=layer, tm=tm_ffn, seq=dec_seq)

    return (xc.reshape(batch, seq, D), xl.reshape(dec_batch, dec_seq, D),
            jnp.stack(new_k, axis=1), jnp.stack(new_v, axis=1))
```
